```python
import jax, jax.numpy as jnp
from jax import lax
import numpy as np

D_MODEL = 4096
BATCH = 2
SEQ = 4096
DEPTH = 2
DEC_BATCH = 1
DEC_SEQ = 16384
PAST_LEN = 128

HEAD_DIM = 128
N_ATTN_HEADS = D_MODEL // (2 * HEAD_DIM)
N_KV_HEADS = N_ATTN_HEADS // 4
N_RET_HEADS = D_MODEL // (2 * HEAD_DIM)
ATTN_WIDTH = N_ATTN_HEADS * HEAD_DIM
KV_WIDTH = N_KV_HEADS * HEAD_DIM
RET_WIDTH = N_RET_HEADS * HEAD_DIM
MIX_WIDTH = ATTN_WIDTH + RET_WIDTH
IN_WIDTH = ATTN_WIDTH + 2 * KV_WIDTH + 4 * RET_WIDTH
D_FF = 256 * ((8 * D_MODEL // 3 + 255) // 256)
N_MOD = 9
GRID_W = 64
Q_BLOCK = 128
RET_CHUNK = 128
ROPE_THETA = 10000.0
NORM_EPS = 1e-6
GN_EPS = 1e-5

kernel_name = 'hymba_attn_retention_macaron_adaln_encoder'


def rms_norm(x, g):
    xf = x.astype(jnp.float32)
    y = xf * lax.rsqrt(jnp.mean(xf * xf, axis=-1, keepdims=True) + NORM_EPS)
    return (y * g.astype(jnp.float32)).astype(x.dtype)


def axial_rope_tables(n_tokens):
    rows = n_tokens // GRID_W
    row = jnp.repeat(jnp.arange(rows, dtype=jnp.float32), GRID_W)
    col = jnp.tile(jnp.arange(GRID_W, dtype=jnp.float32), rows)
    half = HEAD_DIM // 2
    inv = ROPE_THETA ** (-jnp.arange(0, half, 2, dtype=jnp.float32) / half)
    ang = jnp.concatenate([row[:, None] * inv, col[:, None] * inv], axis=-1)
    return jnp.cos(ang), jnp.sin(ang)


def apply_rope(x, cos, sin):
    xf = x.astype(jnp.float32).reshape(*x.shape[:-1], HEAD_DIM // 2, 2)
    x0, x1 = xf[..., 0], xf[..., 1]
    c = cos[None, :, None, :]
    s = sin[None, :, None, :]
    out = jnp.stack([x0 * c - x1 * s, x0 * s + x1 * c], axis=-1).reshape(x.shape)
    return out.astype(x.dtype)


def modulate(h, shift, scale):
    return h * (1.0 + scale[:, None, :]) + shift[:, None, :]


def swiglu(h, w1, w3, w2):
    return (jax.nn.silu(h @ w1) * (h @ w3)) @ w2


def block_attention(q, k, v):
    B, S = q.shape[:2]
    G = N_ATTN_HEADS // N_KV_HEADS
    nb = S // Q_BLOCK
    qb = q.reshape(B, nb, Q_BLOCK, N_KV_HEADS, G, HEAD_DIM).transpose(1, 0, 2, 3, 4, 5)
    scale = HEAD_DIM ** -0.5

    def one_block(qi):
        s = jnp.einsum('bqkgd,bskd->bkgqs', qi, k).astype(jnp.float32) * scale
        p = jax.nn.softmax(s, axis=-1).astype(v.dtype)
        return jnp.einsum('bkgqs,bskd->bqkgd', p, v)

    o = lax.map(one_block, qb)
    return o.transpose(1, 0, 2, 3, 4, 5).reshape(B, S, ATTN_WIDTH)


def retention_chunkwise(q, k, v, log_gamma):
    B, S, H, d = q.shape
    C = RET_CHUNK
    nc = S // C
    qc = q.reshape(B, nc, C, H, d)
    kc = k.reshape(B, nc, C, H, d)
    vc = v.reshape(B, nc, C, H, d)
    pos = jnp.arange(C, dtype=jnp.float32)
    diff = pos[:, None] - pos[None, :]
    causal = diff >= 0
    decay = jnp.where(causal[None], jnp.exp(jnp.where(causal, diff, 0.0)[None] * log_gamma[:, None, None]), 0.0)
    scores = jnp.einsum('bnqhd,bnkhd->bnhqk', qc, kc) * decay[None, None]
    o_inner = jnp.einsum('bnhqk,bnkhe->bnqhe', scores, vc)
    zeta = jnp.exp((C - 1.0 - pos)[None, :] * log_gamma[:, None])
    xi = jnp.exp((pos + 1.0)[None, :] * log_gamma[:, None])
    u = jnp.einsum('bnkhd,hk,bnkhe->nbhde', kc, zeta, vc)
    g_chunk = jnp.exp(C * log_gamma)[None, :, None, None]

    def step(r, u_i):
        return g_chunk * r + u_i, r

    _, r_prev = lax.scan(step, jnp.zeros((B, H, d, d), jnp.float32), u)
    o_cross = jnp.einsum('bnqhd,nbhde,hq->bnqhe', qc, r_prev, xi)
    return (o_inner + o_cross).reshape(B, S, H, d)


def bidirectional_retention(q, k, v, gate, decay_logit_f, decay_logit_b):
    B, S = q.shape[:2]
    qf = q.astype(jnp.float32)
    kf = k.astype(jnp.float32) * (HEAD_DIM ** -0.5)
    vf = v.astype(jnp.float32)
    lg_f = jax.nn.log_sigmoid(decay_logit_f.astype(jnp.float32))
    lg_b = jax.nn.log_sigmoid(decay_logit_b.astype(jnp.float32))
    o_fwd = retention_chunkwise(qf, kf, vf, lg_f)
    o_bwd = jnp.flip(retention_chunkwise(jnp.flip(qf, 1), jnp.flip(kf, 1), jnp.flip(vf, 1), lg_b), 1)
    o = o_fwd + o_bwd
    mu = jnp.mean(o, axis=-1, keepdims=True)
    var = jnp.mean(jnp.square(o - mu), axis=-1, keepdims=True)
    o = ((o - mu) * lax.rsqrt(var + GN_EPS)).reshape(B, S, RET_WIDTH).astype(gate.dtype)
    return jax.nn.silu(gate) * o


def token_mixing(h, w_in, q_norm_g, k_norm_g, decay_logit_f, decay_logit_b, w_out, cos, sin):
    B, S, _ = h.shape
    proj = h @ w_in
    cuts = [ATTN_WIDTH, ATTN_WIDTH + KV_WIDTH, ATTN_WIDTH + 2 * KV_WIDTH,
            ATTN_WIDTH + 2 * KV_WIDTH + RET_WIDTH, ATTN_WIDTH + 2 * KV_WIDTH + 2 * RET_WIDTH,
            ATTN_WIDTH + 2 * KV_WIDTH + 3 * RET_WIDTH]
    aq, ak, av, rq, rk, rv, rg = jnp.split(proj, cuts, axis=-1)
    aq = apply_rope(rms_norm(aq.reshape(B, S, N_ATTN_HEADS, HEAD_DIM), q_norm_g), cos, sin)
    ak = apply_rope(rms_norm(ak.reshape(B, S, N_KV_HEADS, HEAD_DIM), k_norm_g), cos, sin)
    av = av.reshape(B, S, N_KV_HEADS, HEAD_DIM)
    o_attn = block_attention(aq, ak, av)
    rq = apply_rope(rq.reshape(B, S, N_RET_HEADS, HEAD_DIM), cos, sin)
    rk = apply_rope(rk.reshape(B, S, N_RET_HEADS, HEAD_DIM), cos, sin)
    rv = rv.reshape(B, S, N_RET_HEADS, HEAD_DIM)
    o_ret = bidirectional_retention(rq, rk, rv, rg, decay_logit_f, decay_logit_b)
    return jnp.concatenate([o_attn, o_ret], axis=-1) @ w_out


def encoder_layer(x, c, w_mod, b_mod, g_ffn1, ffn1_w1, ffn1_w3, ffn1_w2, g_mix, w_in,
                  q_norm_g, k_norm_g, ret_decay_f, ret_decay_b, w_out, g_ffn2, ffn2_w1,
                  ffn2_w3, ffn2_w2, g_post, cos, sin):
    mod = jax.nn.silu(c) @ w_mod + b_mod
    sh1, sc1, ga1, sh2, sc2, ga2, sh3, sc3, ga3 = jnp.split(mod, N_MOD, axis=-1)
    h = modulate(rms_norm(x, g_ffn1), sh1, sc1)
    x = x + 0.5 * ga1[:, None, :] * swiglu(h, ffn1_w1, ffn1_w3, ffn1_w2)
    h = modulate(rms_norm(x, g_mix), sh2, sc2)
    x = x + ga2[:, None, :] * token_mixing(h, w_in, q_norm_g, k_norm_g, ret_decay_f, ret_decay_b, w_out, cos, sin)
    h = modulate(rms_norm(x, g_ffn2), sh3, sc3)
    x = x + 0.5 * ga3[:, None, :] * swiglu(h, ffn2_w1, ffn2_w3, ffn2_w2)
    return rms_norm(x, g_post)


def run_trunk(x, c, w_mod, b_mod, g_ffn1, ffn1_w1, ffn1_w3, ffn1_w2, g_mix, w_in, q_norm_g,
              k_norm_g, ret_decay_f, ret_decay_b, w_out, g_ffn2, ffn2_w1, ffn2_w3, ffn2_w2, g_post):
    cos, sin = axial_rope_tables(x.shape[1])
    for l in range(DEPTH):
        x = encoder_layer(x, c, w_mod[l], b_mod[l], g_ffn1[l], ffn1_w1[l], ffn1_w3[l], ffn1_w2[l],
                          g_mix[l], w_in[l], q_norm_g[l], k_norm_g[l], ret_decay_f[l], ret_decay_b[l],
                          w_out[l], g_ffn2[l], ffn2_w1[l], ffn2_w3[l], ffn2_w2[l], g_post[l], cos, sin)
    return x


def setup_inputs(seed: int = 0) -> dict:
    key = jax.random.key(seed)
    ks = jax.random.split(key, 24)
    f32 = jnp.float32

    def w(k, shape, fan_in, gain=1.0):
        return jax.random.normal(k, shape, f32) * (gain * fan_in ** -0.5)

    def gains(k, shape):
        return 1.0 + 0.02 * jax.random.normal(k, shape, f32)

    base_logit = jnp.asarray(np.log(2.0 ** (5.0 + np.arange(N_RET_HEADS)) - 1.0), f32)
    return {
        'x_prompt': jax.random.normal(ks[0], (BATCH, SEQ, D_MODEL), f32),
        'x_sample': jax.random.normal(ks[1], (DEC_BATCH, DEC_SEQ, D_MODEL), f32),
        'c_prompt': jax.random.normal(ks[2], (BATCH, D_MODEL), f32),
        'c_sample': jax.random.normal(ks[3], (DEC_BATCH, D_MODEL), f32),
        'w_mod': w(ks[4], (DEPTH, D_MODEL, N_MOD * D_MODEL), D_MODEL, 0.5),
        'b_mod': 0.01 * jax.random.normal(ks[5], (DEPTH, N_MOD * D_MODEL), f32),
        'g_ffn1': gains(ks[6], (DEPTH, D_MODEL)),
        'ffn1_w1': w(ks[7], (DEPTH, D_MODEL, D_FF), D_MODEL),
        'ffn1_w3': w(ks[8], (DEPTH, D_MODEL, D_FF), D_MODEL),
        'ffn1_w2': w(ks[9], (DEPTH, D_FF, D_MODEL), D_FF),
        'g_mix': gains(ks[10], (DEPTH, D_MODEL)),
        'w_in': w(ks[11], (DEPTH, D_MODEL, IN_WIDTH), D_MODEL),
        'q_norm_g': gains(ks[12], (DEPTH, HEAD_DIM)),
        'k_norm_g': gains(ks[13], (DEPTH, HEAD_DIM)),
        'ret_decay_f': base_logit[None, :] + 0.1 * jax.random.normal(ks[14], (DEPTH, N_RET_HEADS), f32),
        'ret_decay_b': base_logit[None, :] + 0.1 * jax.random.normal(ks[15], (DEPTH, N_RET_HEADS), f32),
        'w_out': w(ks[16], (DEPTH, MIX_WIDTH, D_MODEL), MIX_WIDTH),
        'g_ffn2': gains(ks[17], (DEPTH, D_MODEL)),
        'ffn2_w1': w(ks[18], (DEPTH, D_MODEL, D_FF), D_MODEL),
        'ffn2_w3': w(ks[19], (DEPTH, D_MODEL, D_FF), D_MODEL),
        'ffn2_w2': w(ks[20], (DEPTH, D_FF, D_MODEL), D_FF),
        'g_post': gains(ks[21], (DEPTH, D_MODEL)),
    }


def reference(x_prompt, x_sample, c_prompt, c_sample, w_mod, b_mod, g_ffn1, ffn1_w1, ffn1_w3,
              ffn1_w2, g_mix, w_in, q_norm_g, k_norm_g, ret_decay_f, ret_decay_b, w_out, g_ffn2,
              ffn2_w1, ffn2_w3, ffn2_w2, g_post):
    y_prompt = run_trunk(x_prompt, c_prompt, w_mod, b_mod, g_ffn1, ffn1_w1, ffn1_w3, ffn1_w2, g_mix,
                         w_in, q_norm_g, k_norm_g, ret_decay_f, ret_decay_b, w_out, g_ffn2, ffn2_w1,
                         ffn2_w3, ffn2_w2, g_post)
    y_sample = run_trunk(x_sample, c_sample, w_mod, b_mod, g_ffn1, ffn1_w1, ffn1_w3, ffn1_w2, g_mix,
                         w_in, q_norm_g, k_norm_g, ret_decay_f, ret_decay_b, w_out, g_ffn2, ffn2_w1,
                         ffn2_w3, ffn2_w2, g_post)
    return (y_prompt, y_sample)
```

```python
import functools

import jax
import jax.numpy as jnp
from jax import lax
from jax.experimental import pallas as pl
from jax.experimental.pallas import tpu as pltpu

HEAD_DIM = 128
GQA_RATIO = 4
GRID_W = 64
RET_CHUNK = 128
ROPE_THETA = 10000.0
NORM_EPS = 1e-6
GN_EPS = 1e-5
N_MOD = 9
MOD_ROWS = 8

V7X_VMEM_REQUEST_CAP = 60 * 1024 * 1024
VMEM_REQUEST_FLOOR = 16 * 1024 * 1024

F32 = jnp.float32
BF16 = jnp.bfloat16


def _pick(n, prefs):
    for p in prefs:
        if n % p == 0:
            return p
    return n


def _nbytes(shape, dtype):
    size = 1
    for s in shape:
        size *= s
    return size * jnp.dtype(dtype).itemsize


def _params(semantics, blocks, extra=0):
    need = 2 * sum(_nbytes(s, d) for s, d in blocks) + extra + (4 << 20)
    need = max(VMEM_REQUEST_FLOOR, min(V7X_VMEM_REQUEST_CAP, need))
    return pltpu.CompilerParams(dimension_semantics=semantics, vmem_limit_bytes=int(need))


def _silu(x):
    return x * jax.nn.sigmoid(x)


def _rms(x, eps):
    return x * lax.rsqrt(jnp.mean(x * x, axis=-1, keepdims=True) + eps)


class _Layout:
    def __init__(self, groups):
        self.groups = groups
        self.offsets = []
        self.row0 = []
        off = 0
        row = 0
        for n_seq, seq_len in groups:
            assert off % seq_len == 0, "group offset must be a multiple of its sequence length"
            self.offsets.append(off)
            self.row0.append(row)
            off += n_seq * seq_len
            row += n_seq
        self.total = off
        self.n_rows = row
        self.min_seq = min(s for _, s in groups)
        self.max_seq = max(s for _, s in groups)

    def seq_row(self, t0):
        row = None
        for idx in reversed(range(len(self.groups))):
            here = self.row0[idx] + (t0 - self.offsets[idx]) // self.groups[idx][1]
            row = here if row is None else jnp.where(t0 < self.offsets[idx + 1], here, row)
        return row

    def pos(self, t0):
        p = None
        for idx in reversed(range(len(self.groups))):
            here = (t0 - self.offsets[idx]) % self.groups[idx][1]
            p = here if p is None else jnp.where(t0 < self.offsets[idx + 1], here, p)
        return p


def _mod_kernel(c_ref, w_ref, b_ref, o_ref):
    s = _silu(c_ref[...]).astype(BF16)
    o_ref[...] = jnp.dot(s, w_ref[...].astype(BF16), preferred_element_type=F32) + b_ref[...]


def _modulation(c_rows, w_mod, b_mod):
    depth, d, n = w_mod.shape
    tn = _pick(n, (512, 256, 128))
    return pl.pallas_call(
        _mod_kernel,
        out_shape=jax.ShapeDtypeStruct((depth, MOD_ROWS, n), F32),
        grid=(depth, n // tn),
        in_specs=[
            pl.BlockSpec((MOD_ROWS, d), lambda l, j: (0, 0)),
            pl.BlockSpec((None, d, tn), lambda l, j: (l, 0, j)),
            pl.BlockSpec((None, 1, tn), lambda l, j: (l, 0, j)),
        ],
        out_specs=pl.BlockSpec((None, MOD_ROWS, tn), lambda l, j: (l, 0, j)),
        compiler_params=_params(("parallel", "parallel"), [((d, tn), F32), ((MOD_ROWS, d), F32)],
                                extra=_nbytes((d, tn), BF16)),
        name="adaln_mod",
    )(c_rows, w_mod, b_mod.reshape(depth, 1, n))


def _norm_mod_kernel(x_ref, g_ref, sc_ref, sh_ref, h_ref):
    y = _rms(x_ref[...], NORM_EPS) * g_ref[...]
    h_ref[...] = (y * (1.0 + sc_ref[...]) + sh_ref[...]).astype(h_ref.dtype)


def _post_norm_mod_kernel(x_ref, gp_ref, g_ref, sc_ref, sh_ref, xo_ref, h_ref):
    xo = _rms(x_ref[...], NORM_EPS) * gp_ref[...]
    xo_ref[...] = xo
    y = _rms(xo, NORM_EPS) * g_ref[...]
    h_ref[...] = (y * (1.0 + sc_ref[...]) + sh_ref[...]).astype(h_ref.dtype)


def _post_norm_kernel(x_ref, gp_ref, xo_ref):
    xo_ref[...] = _rms(x_ref[...], NORM_EPS) * gp_ref[...]


def _mod_spec(lay, tm, d, chunk):
    return pl.BlockSpec((None, 1, d), lambda i: (lay.seq_row(i * tm), 0, chunk))


def _norm_mod(x, g, mod3, lay, shift_chunk, scale_chunk, g_post=None):
    t, d = x.shape
    tm = _pick(lay.min_seq, (512, 256, 128))
    row = pl.BlockSpec((tm, d), lambda i: (i, 0))
    vec = pl.BlockSpec((1, d), lambda i: (0, 0))
    blocks = [((tm, d), F32), ((tm, d), BF16)]
    if g_post is None:
        return pl.pallas_call(
            _norm_mod_kernel,
            out_shape=jax.ShapeDtypeStruct((t, d), BF16),
            grid=(t // tm,),
            in_specs=[row, vec, _mod_spec(lay, tm, d, scale_chunk), _mod_spec(lay, tm, d, shift_chunk)],
            out_specs=row,
            compiler_params=_params(("parallel",), blocks, extra=2 * _nbytes((tm, d), F32)),
            name="norm_mod",
        )(x, g.reshape(1, d), mod3, mod3)
    return pl.pallas_call(
        _post_norm_mod_kernel,
        out_shape=(jax.ShapeDtypeStruct((t, d), F32), jax.ShapeDtypeStruct((t, d), BF16)),
        grid=(t // tm,),
        in_specs=[row, vec, vec, _mod_spec(lay, tm, d, scale_chunk), _mod_spec(lay, tm, d, shift_chunk)],
        out_specs=(row, row),
        compiler_params=_params(("parallel",), blocks + [((tm, d), F32)], extra=2 * _nbytes((tm, d), F32)),
        name="post_norm_mod",
    )(x, g_post.reshape(1, d), g.reshape(1, d), mod3, mod3)


def _post_norm(x, g_post, t0, n_tok):
    _, d = x.shape
    tm = _pick(n_tok, (512, 256, 128))
    assert t0 % tm == 0
    return pl.pallas_call(
        _post_norm_kernel,
        out_shape=jax.ShapeDtypeStruct((n_tok, d), F32),
        grid=(n_tok // tm,),
        in_specs=[pl.BlockSpec((tm, d), lambda i: (t0 // tm + i, 0)), pl.BlockSpec((1, d), lambda i: (0, 0))],
        out_specs=pl.BlockSpec((tm, d), lambda i: (i, 0)),
        compiler_params=_params(("parallel",), [((tm, d), F32)] * 2, extra=_nbytes((tm, d), F32)),
        name="post_norm",
    )(x, g_post.reshape(1, d))


def _ffn_up_kernel(h_ref, w1_ref, w3_ref, u_ref):
    h = h_ref[...]
    a = jnp.dot(h, w1_ref[...], preferred_element_type=F32)
    b = jnp.dot(h, w3_ref[...], preferred_element_type=F32)
    u_ref[...] = (_silu(a) * b).astype(u_ref.dtype)


def _ffn_up(h, w1, w3, layer):
    t, d = h.shape
    f = w1.shape[-1]
    tm = _pick(t, (1024, 512, 256, 128))
    tn = _pick(f, (512, 256, 128))
    wspec = pl.BlockSpec((None, d, tn), lambda i, j: (layer, 0, j))
    return pl.pallas_call(
        _ffn_up_kernel,
        out_shape=jax.ShapeDtypeStruct((t, f), BF16),
        grid=(t // tm, f // tn),
        in_specs=[pl.BlockSpec((tm, d), lambda i, j: (i, 0)), wspec, wspec],
        out_specs=pl.BlockSpec((tm, tn), lambda i, j: (i, j)),
        compiler_params=_params(("parallel", "arbitrary"),
                                [((tm, d), BF16), ((d, tn), BF16), ((d, tn), BF16), ((tm, tn), BF16)],
                                extra=3 * _nbytes((tm, tn), F32)),
        name="ffn_up",
    )(h, w1, w3)


def _ffn_down_kernel(u_ref, w2_ref, x_ref, ga_ref, o_ref):
    acc = jnp.dot(u_ref[...], w2_ref[...], preferred_element_type=F32)
    o_ref[...] = x_ref[...] + (0.5 * ga_ref[...]) * acc


def _ffn_down(u, w2, x, mod3, lay, layer, gate_chunk):
    t, f = u.shape
    d = x.shape[-1]
    tm = _pick(lay.min_seq, (512, 256, 128))
    tn = _pick(d, (512, 256, 128))
    nj = d // tn
    tile = pl.BlockSpec((tm, tn), lambda i, j: (i, j))
    return pl.pallas_call(
        _ffn_down_kernel,
        out_shape=jax.ShapeDtypeStruct((t, d), F32),
        grid=(t // tm, nj),
        in_specs=[
            pl.BlockSpec((tm, f), lambda i, j: (i, 0)),
            pl.BlockSpec((None, f, tn), lambda i, j: (layer, 0, j)),
            tile,
            pl.BlockSpec((None, 1, tn), lambda i, j: (lay.seq_row(i * tm), 0, gate_chunk * nj + j)),
        ],
        out_specs=tile,
        compiler_params=_params(("parallel", "arbitrary"),
                                [((tm, f), BF16), ((f, tn), BF16), ((tm, tn), F32), ((tm, tn), F32)],
                                extra=2 * _nbytes((tm, tn), F32)),
        name="ffn_down",
    )(u, w2, x, mod3)


def _in_proj_kernel(h_ref, w_ref, o_ref):
    acc = jnp.dot(h_ref[...], w_ref[...], preferred_element_type=F32)
    for c in range(o_ref.shape[0]):
        o_ref[c] = acc[:, c * HEAD_DIM:(c + 1) * HEAD_DIM].astype(o_ref.dtype)


def _in_proj(h, w_in, layer):
    t, d = h.shape
    n = w_in.shape[-1]
    tm = _pick(t, (1024, 512, 256, 128))
    tn = _pick(n, (1024, 512, 256, 128))
    sub = tn // HEAD_DIM
    return pl.pallas_call(
        _in_proj_kernel,
        out_shape=jax.ShapeDtypeStruct((n // HEAD_DIM, t, HEAD_DIM), BF16),
        grid=(t // tm, n // tn),
        in_specs=[pl.BlockSpec((tm, d), lambda i, j: (i, 0)),
                  pl.BlockSpec((None, d, tn), lambda i, j: (layer, 0, j))],
        out_specs=pl.BlockSpec((sub, tm, HEAD_DIM), lambda i, j: (j, i, 0)),
        compiler_params=_params(("parallel", "arbitrary"),
                                [((tm, d), BF16), ((d, tn), BF16), ((tm, tn), BF16)],
                                extra=2 * _nbytes((tm, tn), F32)),
        name="in_proj",
    )(h, w_in)


def _rope_tables(n_pos):
    half = HEAD_DIM // 2
    p = jnp.arange(n_pos, dtype=jnp.int32)
    row = (p // GRID_W).astype(F32)
    col = (p % GRID_W).astype(F32)
    inv = ROPE_THETA ** (-jnp.arange(0, half, 2, dtype=F32) / half)
    ang = jnp.concatenate([row[:, None] * inv, col[:, None] * inv], axis=-1)
    cos = jnp.cos(ang)
    sin = jnp.sin(ang)
    cos2 = jnp.repeat(cos, 2, axis=-1)
    sin2 = jnp.stack([-sin, sin], axis=-1).reshape(n_pos, HEAD_DIM)
    return cos2, sin2


def _qk_prep_kernel(p_ref, cos_ref, sin_ref, gain_ref, flag_ref, o_ref):
    gs, tt, hd = p_ref.shape
    x = p_ref[...].astype(F32).reshape(gs * tt, hd)
    inv = lax.rsqrt(jnp.mean(x * x, axis=-1, keepdims=True) + NORM_EPS)
    y = x * jnp.where(flag_ref[...] > 0.0, inv, 1.0) * gain_ref[...]
    lane = lax.broadcasted_iota(jnp.int32, y.shape, 1)
    partner = jnp.where(lane % 2 == 0, pltpu.roll(y, hd - 1, axis=1), pltpu.roll(y, 1, axis=1))
    y = y.reshape(gs, tt, hd)
    partner = partner.reshape(gs, tt, hd)
    o_ref[...] = (y * cos_ref[...] + partner * sin_ref[...]).astype(o_ref.dtype)


def _qk_prep(proj, cos2, sin2, gain, flag, lay, dims):
    n_attn, n_kv, n_ret, gs = dims
    _, t, hd = proj.shape
    tt = _pick(lay.min_seq, (1024, 512, 256, 128))
    n_groups = (n_attn + n_kv + 2 * n_ret) // gs
    skip_at = (n_attn + n_kv) // gs
    skip = n_kv // gs

    def src(g, i):
        return (jnp.where(g >= skip_at, g + skip, g), i, 0)

    tab = pl.BlockSpec((tt, hd), lambda g, i: (lay.pos(i * tt) // tt, 0))
    par = pl.BlockSpec((None, 1, hd), lambda g, i: (g, 0, 0))
    return pl.pallas_call(
        _qk_prep_kernel,
        out_shape=jax.ShapeDtypeStruct((n_groups * gs, t, hd), BF16),
        grid=(n_groups, t // tt),
        in_specs=[pl.BlockSpec((gs, tt, hd), src), tab, tab, par, par],
        out_specs=pl.BlockSpec((gs, tt, hd), lambda g, i: (g, i, 0)),
        compiler_params=_params(("parallel", "parallel"),
                                [((gs, tt, hd), BF16)] * 2 + [((tt, hd), F32)] * 2,
                                extra=6 * _nbytes((gs, tt, hd), F32)),
        name="qk_prep",
    )(proj, cos2, sin2, gain, flag)


def _attn_kernel(q_ref, k_ref, v_ref, o_ref, m_scr, l_scr, acc_scr, *, tkv):
    g, tq, hd = q_ref.shape
    seq = k_ref.shape[0]
    q = q_ref[...].reshape(g * tq, hd)
    m_scr[...] = jnp.full(m_scr.shape, -jnp.inf, F32)
    l_scr[...] = jnp.zeros(l_scr.shape, F32)
    acc_scr[...] = jnp.zeros(acc_scr.shape, F32)

    def body(c, carry):
        off = pl.multiple_of(c * tkv, tkv)
        k = k_ref[pl.ds(off, tkv), :]
        v = v_ref[pl.ds(off, tkv), :]
        s = lax.dot_general(q, k, (((1,), (1,)), ((), ())), preferred_element_type=F32)
        m_prev = m_scr[...]
        m_new = jnp.maximum(m_prev, jnp.max(s, axis=-1, keepdims=True))
        alpha = jnp.exp(m_prev - m_new)
        p = jnp.exp(s - m_new)
        l_scr[...] = alpha * l_scr[...] + jnp.sum(p, axis=-1, keepdims=True)
        acc_scr[...] = alpha * acc_scr[...] + jnp.dot(p.astype(v.dtype), v, preferred_element_type=F32)
        m_scr[...] = m_new
        return carry

    lax.fori_loop(0, seq // tkv, body, 0)
    out = acc_scr[...] / l_scr[...]
    for h in range(g):
        o_ref[:, h * hd:(h + 1) * hd] = out[h * tq:(h + 1) * tq].astype(o_ref.dtype)


def _attention(qk, proj, lay, dims):
    n_attn, n_kv, n_ret, gs = dims
    _, t, hd = qk.shape
    g = n_attn // n_kv
    out = None
    for (n_seq, seq), off in zip(lay.groups, lay.offsets):
        tq = _pick(seq, (256, 128))
        tkv = _pick(seq, (512, 256, 128))
        nq = seq // tq
        blk0 = off // seq

        def q_map(b, kv, i, nq=nq, off=off):
            return (kv, off // tq + b * nq + i, 0)

        def k_map(b, kv, i, blk0=blk0):
            return (n_attn + kv, blk0 + b, 0)

        def v_map(b, kv, i, blk0=blk0):
            return (n_attn + n_kv + kv, blk0 + b, 0)

        def o_map(b, kv, i, nq=nq, off=off):
            return (off // tq + b * nq + i, kv)

        kernel = functools.partial(_attn_kernel, tkv=tkv)
        args = [qk, qk, proj]
        in_specs = [pl.BlockSpec((g, tq, hd), q_map),
                    pl.BlockSpec((None, seq, hd), k_map),
                    pl.BlockSpec((None, seq, hd), v_map)]
        aliases = {}
        if out is not None:
            args.append(out)
            in_specs.append(pl.BlockSpec(memory_space=pl.ANY))
            aliases = {3: 0}
            kernel = functools.partial(_attn_alias_kernel, tkv=tkv)
        out = pl.pallas_call(
            kernel,
            out_shape=jax.ShapeDtypeStruct((t, n_attn * hd), BF16),
            grid=(n_seq, n_kv, nq),
            in_specs=in_specs,
            out_specs=pl.BlockSpec((tq, g * hd), o_map),
            scratch_shapes=[pltpu.VMEM((g * tq, 1), F32), pltpu.VMEM((g * tq, 1), F32),
                            pltpu.VMEM((g * tq, hd), F32)],
            input_output_aliases=aliases,
            compiler_params=_params(("parallel", "parallel", "arbitrary"),
                                    [((g, tq, hd), BF16), ((seq, hd), BF16), ((seq, hd), BF16),
                                     ((tq, g * hd), BF16)],
                                    extra=3 * _nbytes((g * tq, hd), F32) + 4 * _nbytes((g * tq, tkv), F32)),
            name="attention",
        )(*args)
    return out


def _attn_alias_kernel(q_ref, k_ref, v_ref, prev_ref, o_ref, m_scr, l_scr, acc_scr, *, tkv):
    del prev_ref
    _attn_kernel(q_ref, k_ref, v_ref, o_ref, m_scr, l_scr, acc_scr, tkv=tkv)


def _log_sigmoid(x):
    return -(jnp.maximum(-x, 0.0) + jnp.log1p(jnp.exp(-jnp.abs(x))))


def _ret_kernel(q_ref, k_ref, v_ref, g_ref, dec_ref, o_ref, rb_scr):
    seq, hd = q_ref.shape
    c = RET_CHUNK
    nc = seq // c
    lg = _log_sigmoid(dec_ref[...])
    lg_f = lg[0:1, :]
    lg_b = lg[1:2, :]
    ri = lax.broadcasted_iota(jnp.int32, (c, c), 0).astype(F32)
    ci = lax.broadcasted_iota(jnp.int32, (c, c), 1).astype(F32)
    diff = ri - ci
    decay = (jnp.where(diff >= 0, jnp.exp(jnp.where(diff >= 0, diff, 0.0) * lg_f), 0.0)
             + jnp.where(diff <= 0, jnp.exp(jnp.where(diff <= 0, -diff, 0.0) * lg_b), 0.0))
    lane = lax.broadcasted_iota(jnp.int32, (1, c), 1).astype(F32)
    zeta_f = jnp.exp((c - 1.0 - lane) * lg_f)
    zeta_b = jnp.exp(lane * lg_b)
    xi_f = jnp.exp((ri + 1.0) * lg_f)
    xi_b = jnp.exp((c - ri) * lg_b)
    g_f = jnp.exp(c * lg_f)
    g_b = jnp.exp(c * lg_b)

    def state_update(k, v, zeta):
        kt = k.astype(F32).T
        return jnp.dot((kt * zeta).astype(v.dtype), v, preferred_element_type=F32)

    def backward(i, r):
        n = nc - 1 - i
        off = pl.multiple_of(n * c, c)
        rb_scr[n] = r.astype(rb_scr.dtype)
        k = k_ref[pl.ds(off, c), :]
        v = v_ref[pl.ds(off, c), :]
        return g_b * r + state_update(k, v, zeta_b)

    lax.fori_loop(0, nc, backward, jnp.zeros((hd, hd), F32))

    def forward(n, r):
        off = pl.multiple_of(n * c, c)
        q = q_ref[pl.ds(off, c), :]
        k = k_ref[pl.ds(off, c), :]
        v = v_ref[pl.ds(off, c), :]
        s = lax.dot_general(q, k, (((1,), (1,)), ((), ())), preferred_element_type=F32)
        o = jnp.dot((s * decay).astype(v.dtype), v, preferred_element_type=F32)
        o = o + xi_f * jnp.dot(q, r.astype(q.dtype), preferred_element_type=F32)
        o = o + xi_b * jnp.dot(q, rb_scr[n], preferred_element_type=F32)
        mu = jnp.mean(o, axis=-1, keepdims=True)
        var = jnp.mean(jnp.square(o - mu), axis=-1, keepdims=True)
        on = (o - mu) * lax.rsqrt(var + GN_EPS)
        gate = g_ref[pl.ds(off, c), :].astype(F32)
        o_ref[pl.ds(off, c), :] = (_silu(gate) * on).astype(o_ref.dtype)
        return g_f * r + state_update(k, v, zeta_f)

    lax.fori_loop(0, nc, forward, jnp.zeros((hd, hd), F32))


def _retention(qk, proj, dec, lay, dims):
    n_attn, n_kv, n_ret, gs = dims
    _, t, hd = qk.shape
    q0 = n_attn + n_kv
    k0 = q0 + n_ret
    v0 = n_attn + 2 * n_kv + 2 * n_ret
    g0 = v0 + n_ret
    out = None
    for (n_seq, seq), off in zip(lay.groups, lay.offsets):
        blk0 = off // seq

        def slab(base, blk0=blk0):
            return pl.BlockSpec((None, seq, hd), lambda b, h: (base + h, blk0 + b, 0))

        args = [qk, qk, proj, proj, dec]
        in_specs = [slab(q0), slab(k0), slab(v0), slab(g0),
                    pl.BlockSpec((None, 2, hd), lambda b, h: (h, 0, 0))]
        kernel = _ret_kernel
        aliases = {}
        if out is not None:
            args.append(out)
            in_specs.append(pl.BlockSpec(memory_space=pl.ANY))
            aliases = {5: 0}
            kernel = _ret_alias_kernel
        out = pl.pallas_call(
            kernel,
            out_shape=jax.ShapeDtypeStruct((t, n_ret * hd), BF16),
            grid=(n_seq, n_ret),
            in_specs=in_specs,
            out_specs=pl.BlockSpec((seq, hd), lambda b, h, blk0=blk0: (blk0 + b, h)),
            scratch_shapes=[pltpu.VMEM((seq // RET_CHUNK, hd, hd), BF16)],
            input_output_aliases=aliases,
            compiler_params=_params(("parallel", "parallel"), [((seq, hd), BF16)] * 5,
                                    extra=_nbytes((seq, hd), BF16)),
            name="retention",
        )(*args)
    return out


def _ret_alias_kernel(q_ref, k_ref, v_ref, g_ref, dec_ref, prev_ref, o_ref, rb_scr):
    del prev_ref
    _ret_kernel(q_ref, k_ref, v_ref, g_ref, dec_ref, o_ref, rb_scr)


def _out_proj_kernel(oa_ref, or_ref, wa_ref, wr_ref, x_ref, ga_ref, o_ref):
    acc = jnp.dot(oa_ref[...], wa_ref[...], preferred_element_type=F32)
    acc = acc + jnp.dot(or_ref[...], wr_ref[...], preferred_element_type=F32)
    o_ref[...] = x_ref[...] + ga_ref[...] * acc


def _out_proj(o_attn, o_ret, w_out, x, mod3, lay, layer, gate_chunk):
    t, ka = o_attn.shape
    kr = o_ret.shape[-1]
    assert ka == kr
    d = x.shape[-1]
    tm = _pick(lay.min_seq, (1024, 512, 256, 128))
    tn = _pick(d, (512, 256, 128))
    nj = d // tn
    tile = pl.BlockSpec((tm, tn), lambda i, j: (i, j))
    return pl.pallas_call(
        _out_proj_kernel,
        out_shape=jax.ShapeDtypeStruct((t, d), F32),
        grid=(t // tm, nj),
        in_specs=[
            pl.BlockSpec((tm, ka), lambda i, j: (i, 0)),
            pl.BlockSpec((tm, kr), lambda i, j: (i, 0)),
            pl.BlockSpec((None, ka, tn), lambda i, j: (layer, 0, j)),
            pl.BlockSpec((None, kr, tn), lambda i, j: (layer, 1, j)),
            tile,
            pl.BlockSpec((None, 1, tn), lambda i, j: (lay.seq_row(i * tm), 0, gate_chunk * nj + j)),
        ],
        out_specs=tile,
        compiler_params=_params(("parallel", "arbitrary"),
                                [((tm, ka), BF16), ((tm, kr), BF16), ((ka, tn), BF16), ((kr, tn), BF16),
                                 ((tm, tn), F32), ((tm, tn), F32)],
                                extra=2 * _nbytes((tm, tn), F32)),
        name="out_proj",
    )(o_attn, o_ret, w_out, w_out, x, mod3)


def kernel(x_prompt, x_sample, c_prompt, c_sample, w_mod, b_mod, g_ffn1, ffn1_w1, ffn1_w3, ffn1_w2, g_mix, w_in, q_norm_g, k_norm_g, ret_decay_f, ret_decay_b, w_out, g_ffn2, ffn2_w1, ffn2_w3, ffn2_w2, g_post):
    depth, d, _ = w_mod.shape
    hd = HEAD_DIM
    n_attn = d // (2 * hd)
    n_kv = n_attn // GQA_RATIO
    n_ret = d // (2 * hd)
    gs = 4 if n_kv % 4 == 0 else 1
    dims = (n_attn, n_kv, n_ret, gs)

    named = [(x_prompt, c_prompt), (x_sample, c_sample)]
    order = sorted(range(2), key=lambda i: -named[i][0].shape[1])
    xs = [named[i][0] for i in order]
    cs = [named[i][1] for i in order]
    lay = _Layout([(xg.shape[0], xg.shape[1]) for xg in xs])
    x = jnp.concatenate([xg.reshape(-1, d) for xg in xs], axis=0)
    c_rows = jnp.concatenate(cs + [jnp.zeros((MOD_ROWS - lay.n_rows, d), F32)], axis=0)

    mod = _modulation(c_rows, w_mod, b_mod)
    cos2, sin2 = _rope_tables(lay.max_seq)

    bf = lambda w: w.astype(BF16)
    w1a, w3a, w2a = bf(ffn1_w1), bf(ffn1_w3), bf(ffn1_w2)
    w1b, w3b, w2b = bf(ffn2_w1), bf(ffn2_w3), bf(ffn2_w2)
    w_in_b, w_out_b = bf(w_in), bf(w_out)

    scale = hd ** -0.5
    ones = jnp.ones((hd,), F32)
    h = None
    for l in range(depth):
        mod3 = mod[l].reshape(MOD_ROWS, 1, N_MOD * d)
        gains = ([q_norm_g[l] * scale] * (n_attn // gs) + [k_norm_g[l]] * (n_kv // gs)
                 + [ones] * (n_ret // gs) + [ones * scale] * (n_ret // gs))
        flags = [ones] * ((n_attn + n_kv) // gs) + [0.0 * ones] * (2 * n_ret // gs)
        gain = jnp.stack(gains).reshape(-1, 1, hd)
        flag = jnp.stack(flags).reshape(-1, 1, hd)
        dec = jnp.broadcast_to(jnp.stack([ret_decay_f[l], ret_decay_b[l]], axis=1)[:, :, None],
                               (n_ret, 2, hd)).astype(F32)

        if l == 0:
            h = _norm_mod(x, g_ffn1[l], mod3, lay, 0, 1)
        u = _ffn_up(h, w1a, w3a, l)
        x = _ffn_down(u, w2a, x, mod3, lay, l, 2)

        h = _norm_mod(x, g_mix[l], mod3, lay, 3, 4)
        proj = _in_proj(h, w_in_b, l)
        qk = _qk_prep(proj, cos2, sin2, gain, flag, lay, dims)
        o_attn = _attention(qk, proj, lay, dims)
        o_ret = _retention(qk, proj, dec, lay, dims)
        x = _out_proj(o_attn, o_ret, w_out_b, x, mod3, lay, l, 5)

        h = _norm_mod(x, g_ffn2[l], mod3, lay, 6, 7)
        u = _ffn_up(h, w1b, w3b, l)
        x = _ffn_down(u, w2b, x, mod3, lay, l, 8)

        if l + 1 < depth:
            mod3n = mod[l + 1].reshape(MOD_ROWS, 1, N_MOD * d)
            x, h = _norm_mod(x, g_ffn1[l + 1], mod3n, lay, 0, 1, g_post=g_post[l])

    outs = [None, None]
    for slot, (xg, off) in enumerate(zip(xs, lay.offsets)):
        n_tok = xg.shape[0] * xg.shape[1]
        outs[order[slot]] = _post_norm(x, g_post[depth - 1], off, n_tok).reshape(xg.shape)
    return tuple(outs)
```

```python
import jax
import jax.numpy as jnp
from jax import lax
from jax.experimental import pallas as pl
from jax.experimental.pallas import tpu as pltpu

HEAD_DIM = 128
GQA_RATIO = 4
GRID_W = 64
RET_CHUNK = 128
ROPE_THETA = 10000.0
NORM_EPS = 1e-6
GN_EPS = 1e-5
N_MOD = 9
MOD_ROWS = 8
LOG2_E = 1.4426950408889634
BF16_SUBLANES = 16

ATTN_Q_TILE = 256
ATTN_KV_CHUNK = 512
ATTN_UNROLL = 2
MAX_CHAINS = 8
RET_UNROLL = 8

V7X_VMEM_REQUEST_CAP = 60 * 1024 * 1024
VMEM_REQUEST_FLOOR = 16 * 1024 * 1024

F32 = jnp.float32
BF16 = jnp.bfloat16


def _pick(n, prefs):
    for p in prefs:
        if n % p == 0:
            return p
    return n


def _nbytes(shape, dtype):
    size = 1
    for s in shape:
        size *= s
    return size * jnp.dtype(dtype).itemsize


def _params(semantics, blocks, extra=0):
    need = 2 * sum(_nbytes(s, d) for s, d in blocks) + extra + (4 << 20)
    need = max(VMEM_REQUEST_FLOOR, min(V7X_VMEM_REQUEST_CAP, need))
    return pltpu.CompilerParams(dimension_semantics=semantics, vmem_limit_bytes=int(need))


def _silu(x):
    return x * jax.nn.sigmoid(x)


def _rms(x, eps):
    return x * lax.rsqrt(jnp.mean(x * x, axis=-1, keepdims=True) + eps)


class _Layout:
    def __init__(self, groups):
        self.groups = groups
        self.offsets = []
        self.row0 = []
        off = 0
        row = 0
        for n_seq, seq_len in groups:
            assert off % seq_len == 0, "group offset must be a multiple of its sequence length"
            self.offsets.append(off)
            self.row0.append(row)
            off += n_seq * seq_len
            row += n_seq
        self.total = off
        self.n_rows = row
        self.min_seq = min(s for _, s in groups)
        self.max_seq = max(s for _, s in groups)

    def seq_row(self, t0):
        row = None
        for idx in reversed(range(len(self.groups))):
            here = self.row0[idx] + (t0 - self.offsets[idx]) // self.groups[idx][1]
            row = here if row is None else jnp.where(t0 < self.offsets[idx + 1], here, row)
        return row

    def pos(self, t0):
        p = None
        for idx in reversed(range(len(self.groups))):
            here = (t0 - self.offsets[idx]) % self.groups[idx][1]
            p = here if p is None else jnp.where(t0 < self.offsets[idx + 1], here, p)
        return p


def _mod_kernel(c_ref, w_ref, b_ref, o_ref):
    s = _silu(c_ref[...]).astype(BF16)
    o_ref[...] = jnp.dot(s, w_ref[...].astype(BF16), preferred_element_type=F32) + b_ref[...]


def _modulation(c_rows, w_mod, b_mod):
    depth, d, n = w_mod.shape
    tn = _pick(n, (512, 256, 128))
    return pl.pallas_call(
        _mod_kernel,
        out_shape=jax.ShapeDtypeStruct((depth, MOD_ROWS, n), F32),
        grid=(depth, n // tn),
        in_specs=[
            pl.BlockSpec((MOD_ROWS, d), lambda l, j: (0, 0)),
            pl.BlockSpec((None, d, tn), lambda l, j: (l, 0, j)),
            pl.BlockSpec((None, 1, tn), lambda l, j: (l, 0, j)),
        ],
        out_specs=pl.BlockSpec((None, MOD_ROWS, tn), lambda l, j: (l, 0, j)),
        compiler_params=_params(("parallel", "parallel"), [((d, tn), F32), ((MOD_ROWS, d), F32)],
                                extra=_nbytes((d, tn), BF16)),
        name="adaln_mod",
    )(c_rows, w_mod, b_mod.reshape(depth, 1, n))


def _norm_mod_kernel(x_ref, g_ref, sc_ref, sh_ref, h_ref):
    y = _rms(x_ref[...], NORM_EPS) * g_ref[...]
    h_ref[...] = (y * (1.0 + sc_ref[...]) + sh_ref[...]).astype(h_ref.dtype)


def _post_norm_mod_kernel(x_ref, gp_ref, g_ref, sc_ref, sh_ref, xo_ref, h_ref):
    xo = _rms(x_ref[...], NORM_EPS) * gp_ref[...]
    xo_ref[...] = xo
    y = _rms(xo, NORM_EPS) * g_ref[...]
    h_ref[...] = (y * (1.0 + sc_ref[...]) + sh_ref[...]).astype(h_ref.dtype)


def _post_norm_kernel(x_ref, gp_ref, xo_ref):
    xo_ref[...] = _rms(x_ref[...], NORM_EPS) * gp_ref[...]


def _mod_spec(lay, tm, d, chunk):
    return pl.BlockSpec((None, 1, d), lambda i: (lay.seq_row(i * tm), 0, chunk))


def _norm_mod(x, g, mod3, lay, shift_chunk, scale_chunk, g_post=None):
    t, d = x.shape
    tm = _pick(lay.min_seq, (512, 256, 128))
    row = pl.BlockSpec((tm, d), lambda i: (i, 0))
    vec = pl.BlockSpec((1, d), lambda i: (0, 0))
    blocks = [((tm, d), F32), ((tm, d), BF16)]
    if g_post is None:
        return pl.pallas_call(
            _norm_mod_kernel,
            out_shape=jax.ShapeDtypeStruct((t, d), BF16),
            grid=(t // tm,),
            in_specs=[row, vec, _mod_spec(lay, tm, d, scale_chunk), _mod_spec(lay, tm, d, shift_chunk)],
            out_specs=row,
            compiler_params=_params(("parallel",), blocks, extra=2 * _nbytes((tm, d), F32)),
            name="norm_mod",
        )(x, g.reshape(1, d), mod3, mod3)
    return pl.pallas_call(
        _post_norm_mod_kernel,
        out_shape=(jax.ShapeDtypeStruct((t, d), F32), jax.ShapeDtypeStruct((t, d), BF16)),
        grid=(t // tm,),
        in_specs=[row, vec, vec, _mod_spec(lay, tm, d, scale_chunk), _mod_spec(lay, tm, d, shift_chunk)],
        out_specs=(row, row),
        compiler_params=_params(("parallel",), blocks + [((tm, d), F32)], extra=2 * _nbytes((tm, d), F32)),
        name="post_norm_mod",
    )(x, g_post.reshape(1, d), g.reshape(1, d), mod3, mod3)


def _post_norm(x, g_post, t0, n_tok):
    _, d = x.shape
    tm = _pick(n_tok, (512, 256, 128))
    assert t0 % tm == 0
    return pl.pallas_call(
        _post_norm_kernel,
        out_shape=jax.ShapeDtypeStruct((n_tok, d), F32),
        grid=(n_tok // tm,),
        in_specs=[pl.BlockSpec((tm, d), lambda i: (t0 // tm + i, 0)), pl.BlockSpec((1, d), lambda i: (0, 0))],
        out_specs=pl.BlockSpec((tm, d), lambda i: (i, 0)),
        compiler_params=_params(("parallel",), [((tm, d), F32)] * 2, extra=_nbytes((tm, d), F32)),
        name="post_norm",
    )(x, g_post.reshape(1, d))


def _ffn_up_kernel(h_ref, w1_ref, w3_ref, u_ref):
    h = h_ref[...]
    a = jnp.dot(h, w1_ref[...], preferred_element_type=F32)
    b = jnp.dot(h, w3_ref[...], preferred_element_type=F32)
    u_ref[...] = (_silu(a) * b).astype(u_ref.dtype)


def _ffn_up(h, w1, w3, layer):
    t, d = h.shape
    f = w1.shape[-1]
    tm = _pick(t, (1024, 512, 256, 128))
    tn = _pick(f, (512, 256, 128))
    wspec = pl.BlockSpec((None, d, tn), lambda i, j: (layer, 0, j))
    return pl.pallas_call(
        _ffn_up_kernel,
        out_shape=jax.ShapeDtypeStruct((t, f), BF16),
        grid=(t // tm, f // tn),
        in_specs=[pl.BlockSpec((tm, d), lambda i, j: (i, 0)), wspec, wspec],
        out_specs=pl.BlockSpec((tm, tn), lambda i, j: (i, j)),
        compiler_params=_params(("parallel", "arbitrary"),
                                [((tm, d), BF16), ((d, tn), BF16), ((d, tn), BF16), ((tm, tn), BF16)],
                                extra=3 * _nbytes((tm, tn), F32)),
        name="ffn_up",
    )(h, w1, w3)


def _ffn_down_kernel(u_ref, w2_ref, x_ref, ga_ref, o_ref):
    acc = jnp.dot(u_ref[...], w2_ref[...], preferred_element_type=F32)
    o_ref[...] = x_ref[...] + (0.5 * ga_ref[...]) * acc


def _ffn_down(u, w2, x, mod3, lay, layer, gate_chunk):
    t, f = u.shape
    d = x.shape[-1]
    tm = _pick(lay.min_seq, (512, 256, 128))
    tn = _pick(d, (512, 256, 128))
    nj = d // tn
    tile = pl.BlockSpec((tm, tn), lambda i, j: (i, j))
    return pl.pallas_call(
        _ffn_down_kernel,
        out_shape=jax.ShapeDtypeStruct((t, d), F32),
        grid=(t // tm, nj),
        in_specs=[
            pl.BlockSpec((tm, f), lambda i, j: (i, 0)),
            pl.BlockSpec((None, f, tn), lambda i, j: (layer, 0, j)),
            tile,
            pl.BlockSpec((None, 1, tn), lambda i, j: (lay.seq_row(i * tm), 0, gate_chunk * nj + j)),
        ],
        out_specs=tile,
        compiler_params=_params(("parallel", "arbitrary"),
                                [((tm, f), BF16), ((f, tn), BF16), ((tm, tn), F32), ((tm, tn), F32)],
                                extra=2 * _nbytes((tm, tn), F32)),
        name="ffn_down",
    )(u, w2, x, mod3)


def _in_proj_kernel(h_ref, w_ref, o_ref):
    acc = jnp.dot(h_ref[...], w_ref[...], preferred_element_type=F32)
    for c in range(o_ref.shape[0]):
        o_ref[c] = acc[:, c * HEAD_DIM:(c + 1) * HEAD_DIM].astype(o_ref.dtype)


def _in_proj(h, w_in, layer):
    t, d = h.shape
    n = w_in.shape[-1]
    tm = _pick(t, (1024, 512, 256, 128))
    tn = _pick(n, (1024, 512, 256, 128))
    sub = tn // HEAD_DIM
    return pl.pallas_call(
        _in_proj_kernel,
        out_shape=jax.ShapeDtypeStruct((n // HEAD_DIM, t, HEAD_DIM), BF16),
        grid=(t // tm, n // tn),
        in_specs=[pl.BlockSpec((tm, d), lambda i, j: (i, 0)),
                  pl.BlockSpec((None, d, tn), lambda i, j: (layer, 0, j))],
        out_specs=pl.BlockSpec((sub, tm, HEAD_DIM), lambda i, j: (j, i, 0)),
        compiler_params=_params(("parallel", "arbitrary"),
                                [((tm, d), BF16), ((d, tn), BF16), ((tm, tn), BF16)],
                                extra=2 * _nbytes((tm, tn), F32)),
        name="in_proj",
    )(h, w_in)


def _rope_tables(n_pos):
    half = HEAD_DIM // 2
    p = jnp.arange(n_pos, dtype=jnp.int32)
    row = (p // GRID_W).astype(F32)
    col = (p % GRID_W).astype(F32)
    inv = ROPE_THETA ** (-jnp.arange(0, half, 2, dtype=F32) / half)
    ang = jnp.concatenate([row[:, None] * inv, col[:, None] * inv], axis=-1)
    cos = jnp.cos(ang)
    sin = jnp.sin(ang)
    cos2 = jnp.repeat(cos, 2, axis=-1)
    sin2 = jnp.stack([-sin, sin], axis=-1).reshape(n_pos, HEAD_DIM)
    return cos2, sin2


def _qk_prep_kernel(p_ref, cos_ref, sin_ref, gain_ref, flag_ref, o_ref):
    gs, tt, hd = p_ref.shape
    x = p_ref[...].astype(F32).reshape(gs * tt, hd)
    inv = lax.rsqrt(jnp.mean(x * x, axis=-1, keepdims=True) + NORM_EPS)
    y = x * jnp.where(flag_ref[...] > 0.0, inv, 1.0) * gain_ref[...]
    lane = lax.broadcasted_iota(jnp.int32, y.shape, 1)
    partner = jnp.where(lane % 2 == 0, pltpu.roll(y, hd - 1, axis=1), pltpu.roll(y, 1, axis=1))
    y = y.reshape(gs, tt, hd)
    partner = partner.reshape(gs, tt, hd)
    o_ref[...] = (y * cos_ref[...] + partner * sin_ref[...]).astype(o_ref.dtype)


def _qk_prep(proj, cos2, sin2, gain, flag, lay, dims):
    n_attn, n_kv, n_ret, gs = dims
    _, t, hd = proj.shape
    tt = _pick(lay.min_seq, (1024, 512, 256, 128))
    n_groups = (n_attn + n_kv + 2 * n_ret) // gs
    skip_at = (n_attn + n_kv) // gs
    skip = n_kv // gs

    def src(g, i):
        return (jnp.where(g >= skip_at, g + skip, g), i, 0)

    tab = pl.BlockSpec((tt, hd), lambda g, i: (lay.pos(i * tt) // tt, 0))
    par = pl.BlockSpec((None, 1, hd), lambda g, i: (g, 0, 0))
    return pl.pallas_call(
        _qk_prep_kernel,
        out_shape=jax.ShapeDtypeStruct((n_groups * gs, t, hd), BF16),
        grid=(n_groups, t // tt),
        in_specs=[pl.BlockSpec((gs, tt, hd), src), tab, tab, par, par],
        out_specs=pl.BlockSpec((gs, tt, hd), lambda g, i: (g, i, 0)),
        compiler_params=_params(("parallel", "parallel"),
                                [((gs, tt, hd), BF16)] * 2 + [((tt, hd), F32)] * 2,
                                extra=6 * _nbytes((gs, tt, hd), F32)),
        name="qk_prep",
    )(proj, cos2, sin2, gain, flag)


def _v_transpose_kernel(v_ref, o_ref):
    tkv, hd = v_ref.shape
    o_ref[0:hd, :] = v_ref[...].astype(F32).T.astype(o_ref.dtype)
    o_ref[hd:, :] = jnp.ones((o_ref.shape[0] - hd, tkv), o_ref.dtype)


def _v_transpose(proj, slot0, n_slots, tkv):
    _, t, hd = proj.shape
    rows = hd + BF16_SUBLANES
    return pl.pallas_call(
        _v_transpose_kernel,
        out_shape=jax.ShapeDtypeStruct((n_slots, t // tkv, rows, tkv), proj.dtype),
        grid=(n_slots, t // tkv),
        in_specs=[pl.BlockSpec((None, tkv, hd), lambda s, i: (slot0 + s, i, 0))],
        out_specs=pl.BlockSpec((None, None, rows, tkv), lambda s, i: (s, i, 0, 0)),
        compiler_params=_params(("parallel", "parallel"), [((tkv, hd), BF16), ((rows, tkv), BF16)],
                                extra=2 * _nbytes((tkv, hd), F32)),
        name="v_transpose",
    )(proj)


def _attn_kernel(q_ref, k_ref, vt_ref, o_ref, qt_scr, s_scr, mx_scr, p_scr, a_scr, m_scr, acc_scr):
    g, tq, hd = q_ref.shape
    n_chunks, _, tkv = vt_ref.shape
    qt_scr[...] = q_ref[...].reshape(g * tq, hd).astype(F32).T.astype(qt_scr.dtype)
    m_scr[...] = jnp.full(m_scr.shape, -jnp.inf, F32)
    acc_scr[...] = jnp.zeros(acc_scr.shape, F32)
    p_scr[1] = jnp.zeros(p_scr.shape[1:], p_scr.dtype)
    a_scr[1] = jnp.ones(a_scr.shape[1:], F32)

    def scores(c, slot):
        off = pl.multiple_of(c * tkv, tkv)
        s = jnp.dot(k_ref[pl.ds(off, tkv), :], qt_scr[...], preferred_element_type=F32)
        s_scr[slot] = s
        part = jnp.max(s.reshape(MAX_CHAINS, tkv // MAX_CHAINS, g * tq), axis=0)
        mx_scr[slot] = jnp.max(part, axis=0, keepdims=True)

    def values(c_prev, slot_prev):
        pv = jnp.dot(vt_ref[c_prev], p_scr[slot_prev], preferred_element_type=F32)
        acc_scr[...] = a_scr[slot_prev] * acc_scr[...] + pv

    def step(c, slot):
        values(jnp.maximum(c - 1, 0), 1 - slot)
        scores(jnp.minimum(c + 1, n_chunks - 1), 1 - slot)
        m_prev = m_scr[...]
        m_new = jnp.maximum(m_prev, mx_scr[slot])
        alpha = jnp.exp2(m_prev - m_new)
        p_scr[slot] = jnp.exp2(s_scr[slot] - m_new).astype(p_scr.dtype)
        a_scr[slot] = alpha
        m_scr[...] = m_new

    scores(0, 0)

    unroll = ATTN_UNROLL if n_chunks % ATTN_UNROLL == 0 else 2

    def body(i, carry):
        for u in range(unroll):
            step(unroll * i + u, u % 2)
        return carry

    lax.fori_loop(0, n_chunks // unroll, body, 0)
    if n_chunks % unroll:
        step(jnp.int32(n_chunks - 1), 0)
    values(n_chunks - 1, (n_chunks - 1) % 2)
    out_t = acc_scr[0:hd, :] / acc_scr[hd:hd + 1, :]
    for h in range(g):
        o_ref[:, h * hd:(h + 1) * hd] = out_t[:, h * tq:(h + 1) * tq].T.astype(o_ref.dtype)


def _attention(qk, proj, lay, dims):
    n_attn, n_kv, n_ret, gs = dims
    _, t, hd = qk.shape
    g = n_attn // n_kv
    tkv = _pick(lay.min_seq, (ATTN_KV_CHUNK, 256, 128))
    v_t = _v_transpose(proj, n_attn + n_kv, n_kv, tkv)
    out = None
    for (n_seq, seq), off in zip(lay.groups, lay.offsets):
        tq = _pick(seq, (ATTN_Q_TILE, 128))
        nq = seq // tq
        blk0 = off // seq
        n_chunks = seq // tkv

        def q_map(b, kv, i, nq=nq, off=off):
            return (kv, off // tq + b * nq + i, 0)

        def k_map(b, kv, i, blk0=blk0):
            return (n_attn + kv, blk0 + b, 0)

        def v_map(b, kv, i, blk0=blk0):
            return (kv, blk0 + b, 0, 0)

        def o_map(b, kv, i, nq=nq, off=off):
            return (off // tq + b * nq + i, kv)

        kernel = _attn_kernel
        args = [qk, qk, v_t]
        in_specs = [pl.BlockSpec((g, tq, hd), q_map),
                    pl.BlockSpec((None, seq, hd), k_map),
                    pl.BlockSpec((None, n_chunks, hd + BF16_SUBLANES, tkv), v_map)]
        aliases = {}
        if out is not None:
            args.append(out)
            in_specs.append(pl.BlockSpec(memory_space=pl.ANY))
            aliases = {3: 0}
            kernel = _attn_alias_kernel
        out = pl.pallas_call(
            kernel,
            out_shape=jax.ShapeDtypeStruct((t, n_attn * hd), BF16),
            grid=(n_seq, n_kv, nq),
            in_specs=in_specs,
            out_specs=pl.BlockSpec((tq, g * hd), o_map),
            scratch_shapes=[pltpu.VMEM((hd, g * tq), BF16), pltpu.VMEM((2, tkv, g * tq), F32),
                            pltpu.VMEM((2, 1, g * tq), F32),
                            pltpu.VMEM((2, tkv, g * tq), BF16), pltpu.VMEM((2, 1, g * tq), F32),
                            pltpu.VMEM((1, g * tq), F32),
                            pltpu.VMEM((hd + BF16_SUBLANES, g * tq), F32)],
            input_output_aliases=aliases,
            compiler_params=_params(("parallel", "parallel", "arbitrary"),
                                    [((g, tq, hd), BF16), ((seq, hd), BF16), ((seq, hd), BF16),
                                     ((tq, g * hd), BF16)],
                                    extra=2 * _nbytes((hd, g * tq), F32) + 6 * _nbytes((tkv, g * tq), F32)),
            name="attention",
        )(*args)
    return out


def _attn_alias_kernel(q_ref, k_ref, vt_ref, prev_ref, o_ref, *scratch):
    del prev_ref
    _attn_kernel(q_ref, k_ref, vt_ref, o_ref, *scratch)


def _log_sigmoid(x):
    return -(jnp.maximum(-x, 0.0) + jnp.log1p(jnp.exp(-jnp.abs(x))))


def _ret_kernel(q_ref, k_ref, v_ref, g_ref, dec_ref, o_ref, rb_scr):
    seq, hd = q_ref.shape
    c = RET_CHUNK
    nc = seq // c
    lg = _log_sigmoid(dec_ref[...])
    lg_f = lg[0:1, :]
    lg_b = lg[1:2, :]
    ri = lax.broadcasted_iota(jnp.int32, (c, c), 0).astype(F32)
    ci = lax.broadcasted_iota(jnp.int32, (c, c), 1).astype(F32)
    diff = ri - ci
    decay = (jnp.where(diff >= 0, jnp.exp(jnp.where(diff >= 0, diff, 0.0) * lg_f), 0.0)
             + jnp.where(diff <= 0, jnp.exp(jnp.where(diff <= 0, -diff, 0.0) * lg_b), 0.0))
    lane = lax.broadcasted_iota(jnp.int32, (1, c), 1).astype(F32)
    zeta_f = jnp.exp((c - 1.0 - lane) * lg_f)
    zeta_b = jnp.exp(lane * lg_b)
    xi_f = jnp.exp((ri + 1.0) * lg_f)
    xi_b = jnp.exp((c - ri) * lg_b)
    g_f = jnp.exp(c * lg_f)
    g_b = jnp.exp(c * lg_b)

    def state_update(k, v, zeta):
        kt = k.astype(F32).T
        return jnp.dot((kt * zeta).astype(v.dtype), v, preferred_element_type=F32)

    def backward(i, r):
        n = nc - 1 - i
        off = pl.multiple_of(n * c, c)
        rb_scr[n] = r.astype(rb_scr.dtype)
        k = k_ref[pl.ds(off, c), :]
        v = v_ref[pl.ds(off, c), :]
        return g_b * r + state_update(k, v, zeta_b)

    lax.fori_loop(0, nc, backward, jnp.zeros((hd, hd), F32), unroll=RET_UNROLL)

    def forward(n, r):
        off = pl.multiple_of(n * c, c)
        q = q_ref[pl.ds(off, c), :]
        k = k_ref[pl.ds(off, c), :]
        v = v_ref[pl.ds(off, c), :]
        s = lax.dot_general(q, k, (((1,), (1,)), ((), ())), preferred_element_type=F32)
        o = jnp.dot((s * decay).astype(v.dtype), v, preferred_element_type=F32)
        o = o + xi_f * jnp.dot(q, r.astype(q.dtype), preferred_element_type=F32)
        o = o + xi_b * jnp.dot(q, rb_scr[n], preferred_element_type=F32)
        mu = jnp.mean(o, axis=-1, keepdims=True)
        var = jnp.mean(jnp.square(o - mu), axis=-1, keepdims=True)
        on = (o - mu) * lax.rsqrt(var + GN_EPS)
        gate = g_ref[pl.ds(off, c), :].astype(F32)
        o_ref[pl.ds(off, c), :] = (_silu(gate) * on).astype(o_ref.dtype)
        return g_f * r + state_update(k, v, zeta_f)

    lax.fori_loop(0, nc, forward, jnp.zeros((hd, hd), F32), unroll=RET_UNROLL)


def _retention(qk, proj, dec, lay, dims):
    n_attn, n_kv, n_ret, gs = dims
    _, t, hd = qk.shape
    q0 = n_attn + n_kv
    k0 = q0 + n_ret
    v0 = n_attn + 2 * n_kv + 2 * n_ret
    g0 = v0 + n_ret
    out = None
    for (n_seq, seq), off in zip(lay.groups, lay.offsets):
        blk0 = off // seq

        def slab(base, blk0=blk0):
            return pl.BlockSpec((None, seq, hd), lambda b, h: (base + h, blk0 + b, 0))

        args = [qk, qk, proj, proj, dec]
        in_specs = [slab(q0), slab(k0), slab(v0), slab(g0),
                    pl.BlockSpec((None, 2, hd), lambda b, h: (h, 0, 0))]
        kernel = _ret_kernel
        aliases = {}
        if out is not None:
            args.append(out)
            in_specs.append(pl.BlockSpec(memory_space=pl.ANY))
            aliases = {5: 0}
            kernel = _ret_alias_kernel
        out = pl.pallas_call(
            kernel,
            out_shape=jax.ShapeDtypeStruct((t, n_ret * hd), BF16),
            grid=(n_seq, n_ret),
            in_specs=in_specs,
            out_specs=pl.BlockSpec((seq, hd), lambda b, h, blk0=blk0: (blk0 + b, h)),
            scratch_shapes=[pltpu.VMEM((seq // RET_CHUNK, hd, hd), BF16)],
            input_output_aliases=aliases,
            compiler_params=_params(("parallel", "parallel"), [((seq, hd), BF16)] * 5,
                                    extra=_nbytes((seq, hd), BF16)),
            name="retention",
        )(*args)
    return out


def _ret_alias_kernel(q_ref, k_ref, v_ref, g_ref, dec_ref, prev_ref, o_ref, rb_scr):
    del prev_ref
    _ret_kernel(q_ref, k_ref, v_ref, g_ref, dec_ref, o_ref, rb_scr)


def _out_proj_kernel(oa_ref, or_ref, wa_ref, wr_ref, x_ref, ga_ref, o_ref):
    acc = jnp.dot(oa_ref[...], wa_ref[...], preferred_element_type=F32)
    acc = acc + jnp.dot(or_ref[...], wr_ref[...], preferred_element_type=F32)
    o_ref[...] = x_ref[...] + ga_ref[...] * acc


def _out_proj(o_attn, o_ret, w_out, x, mod3, lay, layer, gate_chunk):
    t, ka = o_attn.shape
    kr = o_ret.shape[-1]
    assert ka == kr
    d = x.shape[-1]
    tm = _pick(lay.min_seq, (1024, 512, 256, 128))
    tn = _pick(d, (512, 256, 128))
    nj = d // tn
    tile = pl.BlockSpec((tm, tn), lambda i, j: (i, j))
    return pl.pallas_call(
        _out_proj_kernel,
        out_shape=jax.ShapeDtypeStruct((t, d), F32),
        grid=(t // tm, nj),
        in_specs=[
            pl.BlockSpec((tm, ka), lambda i, j: (i, 0)),
            pl.BlockSpec((tm, kr), lambda i, j: (i, 0)),
            pl.BlockSpec((None, ka, tn), lambda i, j: (layer, 0, j)),
            pl.BlockSpec((None, kr, tn), lambda i, j: (layer, 1, j)),
            tile,
            pl.BlockSpec((None, 1, tn), lambda i, j: (lay.seq_row(i * tm), 0, gate_chunk * nj + j)),
        ],
        out_specs=tile,
        compiler_params=_params(("parallel", "arbitrary"),
                                [((tm, ka), BF16), ((tm, kr), BF16), ((ka, tn), BF16), ((kr, tn), BF16),
                                 ((tm, tn), F32), ((tm, tn), F32)],
                                extra=2 * _nbytes((tm, tn), F32)),
        name="out_proj",
    )(o_attn, o_ret, w_out, w_out, x, mod3)


def kernel(x_prompt, x_sample, c_prompt, c_sample, w_mod, b_mod, g_ffn1, ffn1_w1, ffn1_w3, ffn1_w2, g_mix, w_in, q_norm_g, k_norm_g, ret_decay_f, ret_decay_b, w_out, g_ffn2, ffn2_w1, ffn2_w3, ffn2_w2, g_post):
    depth, d, _ = w_mod.shape
    hd = HEAD_DIM
    n_attn = d // (2 * hd)
    n_kv = n_attn // GQA_RATIO
    n_ret = d // (2 * hd)
    gs = 4 if n_kv % 4 == 0 else 1
    dims = (n_attn, n_kv, n_ret, gs)

    named = [(x_prompt, c_prompt), (x_sample, c_sample)]
    order = sorted(range(2), key=lambda i: -named[i][0].shape[1])
    xs = [named[i][0] for i in order]
    cs = [named[i][1] for i in order]
    lay = _Layout([(xg.shape[0], xg.shape[1]) for xg in xs])
    x = jnp.concatenate([xg.reshape(-1, d) for xg in xs], axis=0)
    c_rows = jnp.concatenate(cs + [jnp.zeros((MOD_ROWS - lay.n_rows, d), F32)], axis=0)

    mod = _modulation(c_rows, w_mod, b_mod)
    cos2, sin2 = _rope_tables(lay.max_seq)

    bf = lambda w: w.astype(BF16)
    w1a, w3a, w2a = bf(ffn1_w1), bf(ffn1_w3), bf(ffn1_w2)
    w1b, w3b, w2b = bf(ffn2_w1), bf(ffn2_w3), bf(ffn2_w2)
    w_in_b, w_out_b = bf(w_in), bf(w_out)

    scale = hd ** -0.5
    ones = jnp.ones((hd,), F32)
    h = None
    for l in range(depth):
        mod3 = mod[l].reshape(MOD_ROWS, 1, N_MOD * d)
        gains = ([q_norm_g[l] * (scale * LOG2_E)] * (n_attn // gs) + [k_norm_g[l]] * (n_kv // gs)
                 + [ones] * (n_ret // gs) + [ones * scale] * (n_ret // gs))
        flags = [ones] * ((n_attn + n_kv) // gs) + [0.0 * ones] * (2 * n_ret // gs)
        gain = jnp.stack(gains).reshape(-1, 1, hd)
        flag = jnp.stack(flags).reshape(-1, 1, hd)
        dec = jnp.broadcast_to(jnp.stack([ret_decay_f[l], ret_decay_b[l]], axis=1)[:, :, None],
                               (n_ret, 2, hd)).astype(F32)

        if l == 0:
            h = _norm_mod(x, g_ffn1[l], mod3, lay, 0, 1)
        u = _ffn_up(h, w1a, w3a, l)
        x = _ffn_down(u, w2a, x, mod3, lay, l, 2)

        h = _norm_mod(x, g_mix[l], mod3, lay, 3, 4)
        proj = _in_proj(h, w_in_b, l)
        qk = _qk_prep(proj, cos2, sin2, gain, flag, lay, dims)
        o_attn = _attention(qk, proj, lay, dims)
        o_ret = _retention(qk, proj, dec, lay, dims)
        x = _out_proj(o_attn, o_ret, w_out_b, x, mod3, lay, l, 5)

        h = _norm_mod(x, g_ffn2[l], mod3, lay, 6, 7)
        u = _ffn_up(h, w1b, w3b, l)
        x = _ffn_down(u, w2b, x, mod3, lay, l, 8)

        if l + 1 < depth:
            mod3n = mod[l + 1].reshape(MOD_ROWS, 1, N_MOD * d)
            x, h = _norm_mod(x, g_ffn1[l + 1], mod3n, lay, 0, 1, g_post=g_post[l])

    outs = [None, None]
    for slot, (xg, off) in enumerate(zip(xs, lay.offsets)):
        n_tok = xg.shape[0] * xg.shape[1]
        outs[order[slot]] = _post_norm(x, g_post[depth - 1], off, n_tok).reshape(xg.shape)
    return tuple(outs)
```

```python
import jax
import jax.numpy as jnp
from jax import lax
from jax.experimental import pallas as pl
from jax.experimental.pallas import tpu as pltpu

HEAD_DIM = 128
GQA_RATIO = 4
GRID_W = 64
RET_CHUNK = 128
ROPE_THETA = 10000.0
NORM_EPS = 1e-6
GN_EPS = 1e-5
N_MOD = 9
MOD_ROWS = 8
LOG2_E = 1.4426950408889634
BF16_SUBLANES = 16

ATTN_Q_TILE = 512
ATTN_KV_CHUNK = 512
ATTN_UNROLL = 2
ATTN_FAST_BOUND = 40.0
RET_BLOCK = 16

V7X_VMEM_REQUEST_CAP = 60 * 1024 * 1024
VMEM_REQUEST_FLOOR = 16 * 1024 * 1024

F32 = jnp.float32
BF16 = jnp.bfloat16


def _pick(n, prefs):
    for p in prefs:
        if n % p == 0:
            return p
    return n


def _nbytes(shape, dtype):
    size = 1
    for s in shape:
        size *= s
    return size * jnp.dtype(dtype).itemsize


def _params(semantics, blocks, extra=0):
    need = 2 * sum(_nbytes(s, d) for s, d in blocks) + extra + (4 << 20)
    need = max(VMEM_REQUEST_FLOOR, min(V7X_VMEM_REQUEST_CAP, need))
    return pltpu.CompilerParams(dimension_semantics=semantics, vmem_limit_bytes=int(need))


def _silu(x):
    return x * jax.nn.sigmoid(x)


def _rms(x, eps):
    return x * lax.rsqrt(jnp.mean(x * x, axis=-1, keepdims=True) + eps)


class _Layout:
    def __init__(self, groups):
        self.groups = groups
        self.offsets = []
        self.row0 = []
        off = 0
        row = 0
        for n_seq, seq_len in groups:
            assert off % seq_len == 0, "group offset must be a multiple of its sequence length"
            self.offsets.append(off)
            self.row0.append(row)
            off += n_seq * seq_len
            row += n_seq
        self.total = off
        self.n_rows = row
        self.min_seq = min(s for _, s in groups)
        self.max_seq = max(s for _, s in groups)

    def seq_row(self, t0):
        row = None
        for idx in reversed(range(len(self.groups))):
            here = self.row0[idx] + (t0 - self.offsets[idx]) // self.groups[idx][1]
            row = here if row is None else jnp.where(t0 < self.offsets[idx + 1], here, row)
        return row

    def pos(self, t0):
        p = None
        for idx in reversed(range(len(self.groups))):
            here = (t0 - self.offsets[idx]) % self.groups[idx][1]
            p = here if p is None else jnp.where(t0 < self.offsets[idx + 1], here, p)
        return p


def _mod_kernel(c_ref, w_ref, b_ref, o_ref):
    s = _silu(c_ref[...]).astype(BF16)
    o_ref[...] = jnp.dot(s, w_ref[...].astype(BF16), preferred_element_type=F32) + b_ref[...]


def _modulation(c_rows, w_mod, b_mod):
    depth, d, n = w_mod.shape
    tn = _pick(n, (512, 256, 128))
    return pl.pallas_call(
        _mod_kernel,
        out_shape=jax.ShapeDtypeStruct((depth, MOD_ROWS, n), F32),
        grid=(depth, n // tn),
        in_specs=[
            pl.BlockSpec((MOD_ROWS, d), lambda l, j: (0, 0)),
            pl.BlockSpec((None, d, tn), lambda l, j: (l, 0, j)),
            pl.BlockSpec((None, 1, tn), lambda l, j: (l, 0, j)),
        ],
        out_specs=pl.BlockSpec((None, MOD_ROWS, tn), lambda l, j: (l, 0, j)),
        compiler_params=_params(("parallel", "parallel"), [((d, tn), F32), ((MOD_ROWS, d), F32)],
                                extra=_nbytes((d, tn), BF16)),
        name="adaln_mod",
    )(c_rows, w_mod, b_mod.reshape(depth, 1, n))


def _norm_mod_kernel(x_ref, g_ref, sc_ref, sh_ref, h_ref):
    y = _rms(x_ref[...], NORM_EPS) * g_ref[...]
    h_ref[...] = (y * (1.0 + sc_ref[...]) + sh_ref[...]).astype(h_ref.dtype)


def _post_norm_mod_kernel(x_ref, gp_ref, g_ref, sc_ref, sh_ref, xo_ref, h_ref):
    xo = _rms(x_ref[...], NORM_EPS) * gp_ref[...]
    xo_ref[...] = xo
    y = _rms(xo, NORM_EPS) * g_ref[...]
    h_ref[...] = (y * (1.0 + sc_ref[...]) + sh_ref[...]).astype(h_ref.dtype)


def _post_norm_kernel(x_ref, gp_ref, xo_ref):
    xo_ref[...] = _rms(x_ref[...], NORM_EPS) * gp_ref[...]


def _mod_spec(lay, tm, d, chunk):
    return pl.BlockSpec((None, 1, d), lambda i: (lay.seq_row(i * tm), 0, chunk))


def _norm_mod(x, g, mod3, lay, shift_chunk, scale_chunk, g_post=None):
    t, d = x.shape
    tm = _pick(lay.min_seq, (512, 256, 128))
    row = pl.BlockSpec((tm, d), lambda i: (i, 0))
    vec = pl.BlockSpec((1, d), lambda i: (0, 0))
    blocks = [((tm, d), F32), ((tm, d), BF16)]
    if g_post is None:
        return pl.pallas_call(
            _norm_mod_kernel,
            out_shape=jax.ShapeDtypeStruct((t, d), BF16),
            grid=(t // tm,),
            in_specs=[row, vec, _mod_spec(lay, tm, d, scale_chunk), _mod_spec(lay, tm, d, shift_chunk)],
            out_specs=row,
            compiler_params=_params(("parallel",), blocks, extra=2 * _nbytes((tm, d), F32)),
            name="norm_mod",
        )(x, g.reshape(1, d), mod3, mod3)
    return pl.pallas_call(
        _post_norm_mod_kernel,
        out_shape=(jax.ShapeDtypeStruct((t, d), F32), jax.ShapeDtypeStruct((t, d), BF16)),
        grid=(t // tm,),
        in_specs=[row, vec, vec, _mod_spec(lay, tm, d, scale_chunk), _mod_spec(lay, tm, d, shift_chunk)],
        out_specs=(row, row),
        compiler_params=_params(("parallel",), blocks + [((tm, d), F32)], extra=2 * _nbytes((tm, d), F32)),
        name="post_norm_mod",
    )(x, g_post.reshape(1, d), g.reshape(1, d), mod3, mod3)


def _post_norm(x, g_post, t0, n_tok):
    _, d = x.shape
    tm = _pick(n_tok, (512, 256, 128))
    assert t0 % tm == 0
    return pl.pallas_call(
        _post_norm_kernel,
        out_shape=jax.ShapeDtypeStruct((n_tok, d), F32),
        grid=(n_tok // tm,),
        in_specs=[pl.BlockSpec((tm, d), lambda i: (t0 // tm + i, 0)), pl.BlockSpec((1, d), lambda i: (0, 0))],
        out_specs=pl.BlockSpec((tm, d), lambda i: (i, 0)),
        compiler_params=_params(("parallel",), [((tm, d), F32)] * 2, extra=_nbytes((tm, d), F32)),
        name="post_norm",
    )(x, g_post.reshape(1, d))


def _ffn_up_kernel(h_ref, w1_ref, w3_ref, u_ref):
    h = h_ref[...]
    a = jnp.dot(h, w1_ref[...], preferred_element_type=F32)
    b = jnp.dot(h, w3_ref[...], preferred_element_type=F32)
    u_ref[...] = (_silu(a) * b).astype(u_ref.dtype)


def _ffn_up(h, w1, w3, layer):
    t, d = h.shape
    f = w1.shape[-1]
    tm = _pick(t, (1024, 512, 256, 128))
    tn = _pick(f, (512, 256, 128))
    wspec = pl.BlockSpec((None, d, tn), lambda i, j: (layer, 0, j))
    return pl.pallas_call(
        _ffn_up_kernel,
        out_shape=jax.ShapeDtypeStruct((t, f), BF16),
        grid=(t // tm, f // tn),
        in_specs=[pl.BlockSpec((tm, d), lambda i, j: (i, 0)), wspec, wspec],
        out_specs=pl.BlockSpec((tm, tn), lambda i, j: (i, j)),
        compiler_params=_params(("parallel", "arbitrary"),
                                [((tm, d), BF16), ((d, tn), BF16), ((d, tn), BF16), ((tm, tn), BF16)],
                                extra=3 * _nbytes((tm, tn), F32)),
        name="ffn_up",
    )(h, w1, w3)


def _ffn_down_kernel(u_ref, w2_ref, x_ref, ga_ref, o_ref):
    acc = jnp.dot(u_ref[...], w2_ref[...], preferred_element_type=F32)
    o_ref[...] = x_ref[...] + (0.5 * ga_ref[...]) * acc


def _ffn_down(u, w2, x, mod3, lay, layer, gate_chunk):
    t, f = u.shape
    d = x.shape[-1]
    tm = _pick(lay.min_seq, (512, 256, 128))
    tn = _pick(d, (512, 256, 128))
    nj = d // tn
    tile = pl.BlockSpec((tm, tn), lambda i, j: (i, j))
    return pl.pallas_call(
        _ffn_down_kernel,
        out_shape=jax.ShapeDtypeStruct((t, d), F32),
        grid=(t // tm, nj),
        in_specs=[
            pl.BlockSpec((tm, f), lambda i, j: (i, 0)),
            pl.BlockSpec((None, f, tn), lambda i, j: (layer, 0, j)),
            tile,
            pl.BlockSpec((None, 1, tn), lambda i, j: (lay.seq_row(i * tm), 0, gate_chunk * nj + j)),
        ],
        out_specs=tile,
        compiler_params=_params(("parallel", "arbitrary"),
                                [((tm, f), BF16), ((f, tn), BF16), ((tm, tn), F32), ((tm, tn), F32)],
                                extra=2 * _nbytes((tm, tn), F32)),
        name="ffn_down",
    )(u, w2, x, mod3)


def _in_proj_kernel(h_ref, w_ref, o_ref):
    acc = jnp.dot(h_ref[...], w_ref[...], preferred_element_type=F32)
    for c in range(o_ref.shape[0]):
        o_ref[c] = acc[:, c * HEAD_DIM:(c + 1) * HEAD_DIM].astype(o_ref.dtype)


def _in_proj(h, w_in, layer):
    t, d = h.shape
    n = w_in.shape[-1]
    tm = _pick(t, (1024, 512, 256, 128))
    tn = _pick(n, (1024, 512, 256, 128))
    sub = tn // HEAD_DIM
    return pl.pallas_call(
        _in_proj_kernel,
        out_shape=jax.ShapeDtypeStruct((n // HEAD_DIM, t, HEAD_DIM), BF16),
        grid=(t // tm, n // tn),
        in_specs=[pl.BlockSpec((tm, d), lambda i, j: (i, 0)),
                  pl.BlockSpec((None, d, tn), lambda i, j: (layer, 0, j))],
        out_specs=pl.BlockSpec((sub, tm, HEAD_DIM), lambda i, j: (j, i, 0)),
        compiler_params=_params(("parallel", "arbitrary"),
                                [((tm, d), BF16), ((d, tn), BF16), ((tm, tn), BF16)],
                                extra=2 * _nbytes((tm, tn), F32)),
        name="in_proj",
    )(h, w_in)


def _rope_tables(n_pos):
    half = HEAD_DIM // 2
    p = jnp.arange(n_pos, dtype=jnp.int32)
    row = (p // GRID_W).astype(F32)
    col = (p % GRID_W).astype(F32)
    inv = ROPE_THETA ** (-jnp.arange(0, half, 2, dtype=F32) / half)
    ang = jnp.concatenate([row[:, None] * inv, col[:, None] * inv], axis=-1)
    cos = jnp.cos(ang)
    sin = jnp.sin(ang)
    cos2 = jnp.repeat(cos, 2, axis=-1)
    sin2 = jnp.stack([-sin, sin], axis=-1).reshape(n_pos, HEAD_DIM)
    return cos2, sin2


def _qk_prep_kernel(p_ref, cos_ref, sin_ref, gain_ref, flag_ref, o_ref):
    gs, tt, hd = p_ref.shape
    x = p_ref[...].astype(F32).reshape(gs * tt, hd)
    inv = lax.rsqrt(jnp.mean(x * x, axis=-1, keepdims=True) + NORM_EPS)
    y = x * jnp.where(flag_ref[...] > 0.0, inv, 1.0) * gain_ref[...]
    lane = lax.broadcasted_iota(jnp.int32, y.shape, 1)
    partner = jnp.where(lane % 2 == 0, pltpu.roll(y, hd - 1, axis=1), pltpu.roll(y, 1, axis=1))
    y = y.reshape(gs, tt, hd)
    partner = partner.reshape(gs, tt, hd)
    o_ref[...] = (y * cos_ref[...] + partner * sin_ref[...]).astype(o_ref.dtype)


def _qk_prep(proj, cos2, sin2, gain, flag, lay, dims):
    n_attn, n_kv, n_ret, gs = dims
    _, t, hd = proj.shape
    tt = _pick(lay.min_seq, (1024, 512, 256, 128))
    n_groups = (n_attn + n_kv + 2 * n_ret) // gs
    skip_at = (n_attn + n_kv) // gs
    skip = n_kv // gs

    def src(g, i):
        return (jnp.where(g >= skip_at, g + skip, g), i, 0)

    tab = pl.BlockSpec((tt, hd), lambda g, i: (lay.pos(i * tt) // tt, 0))
    par = pl.BlockSpec((None, 1, hd), lambda g, i: (g, 0, 0))
    return pl.pallas_call(
        _qk_prep_kernel,
        out_shape=jax.ShapeDtypeStruct((n_groups * gs, t, hd), BF16),
        grid=(n_groups, t // tt),
        in_specs=[pl.BlockSpec((gs, tt, hd), src), tab, tab, par, par],
        out_specs=pl.BlockSpec((gs, tt, hd), lambda g, i: (g, i, 0)),
        compiler_params=_params(("parallel", "parallel"),
                                [((gs, tt, hd), BF16)] * 2 + [((tt, hd), F32)] * 2,
                                extra=6 * _nbytes((gs, tt, hd), F32)),
        name="qk_prep",
    )(proj, cos2, sin2, gain, flag)


def _v_transpose_kernel(v_ref, o_ref):
    tkv, hd = v_ref.shape
    o_ref[0:hd, :] = v_ref[...].astype(F32).T.astype(o_ref.dtype)
    o_ref[hd:, :] = jnp.ones((o_ref.shape[0] - hd, tkv), o_ref.dtype)


def _v_transpose(proj, slot0, n_slots, tkv):
    _, t, hd = proj.shape
    rows = hd + BF16_SUBLANES
    return pl.pallas_call(
        _v_transpose_kernel,
        out_shape=jax.ShapeDtypeStruct((n_slots, t // tkv, rows, tkv), proj.dtype),
        grid=(n_slots, t // tkv),
        in_specs=[pl.BlockSpec((None, tkv, hd), lambda s, i: (slot0 + s, i, 0))],
        out_specs=pl.BlockSpec((None, None, rows, tkv), lambda s, i: (s, i, 0, 0)),
        compiler_params=_params(("parallel", "parallel"), [((tkv, hd), BF16), ((rows, tkv), BF16)],
                                extra=2 * _nbytes((tkv, hd), F32)),
        name="v_transpose",
    )(proj)


def _attn_kernel(q_ref, k_ref, vt_ref, o_ref, qt_scr, s_scr, mx_scr, p_scr, a_scr, m_scr, acc_scr, kmax_scr):
    g, tq, hd = q_ref.shape
    n_chunks, _, tkv = vt_ref.shape
    nq = g * tq
    qt_scr[...] = q_ref[...].reshape(nq, hd).astype(F32).T.astype(qt_scr.dtype)
    acc_scr[...] = jnp.zeros(acc_scr.shape, F32)
    p_scr[1] = jnp.zeros(p_scr.shape[1:], p_scr.dtype)

    def key_chunk(c):
        return k_ref[pl.ds(pl.multiple_of(c * tkv, tkv), tkv), :]

    @pl.when(pl.program_id(2) == 0)
    def _():
        def chunk_max(c, mx):
            kf = key_chunk(c).astype(F32)
            return jnp.maximum(mx, jnp.max(jnp.sum(kf * kf, axis=1, keepdims=True), axis=0, keepdims=True))

        kmax_scr[...] = lax.fori_loop(0, n_chunks, chunk_max, jnp.zeros((1, 1), F32))

    qf = qt_scr[...].astype(F32)
    bound = jnp.sqrt(jnp.sum(qf * qf, axis=0, keepdims=True) * kmax_scr[...])
    small = jnp.max(bound) < ATTN_FAST_BOUND

    def run_chunks(step):
        unroll = ATTN_UNROLL if n_chunks % ATTN_UNROLL == 0 else 2

        def body(i, carry):
            for u in range(unroll):
                step(unroll * i + u, u % 2)
            return carry

        lax.fori_loop(0, n_chunks // unroll, body, 0)
        if n_chunks % unroll:
            step(jnp.int32(n_chunks - 1), 0)

    @pl.when(small)
    def _():
        def step(c, slot):
            s = jnp.dot(key_chunk(c), qt_scr[...], preferred_element_type=F32)
            p_scr[slot] = jnp.exp2(s - bound).astype(p_scr.dtype)
            acc_scr[...] += jnp.dot(vt_ref[jnp.maximum(c - 1, 0)], p_scr[1 - slot],
                                    preferred_element_type=F32)

        run_chunks(step)
        last = (n_chunks - 1) % 2
        acc_scr[...] += jnp.dot(vt_ref[n_chunks - 1], p_scr[last], preferred_element_type=F32)

    @pl.when(jnp.logical_not(small))
    def _():
        m_scr[...] = jnp.full(m_scr.shape, -jnp.inf, F32)
        a_scr[1] = jnp.ones(a_scr.shape[1:], F32)

        def scores(c, slot):
            s = jnp.dot(key_chunk(c), qt_scr[...], preferred_element_type=F32)
            s_scr[slot] = s
            mx_scr[slot] = jnp.max(s, axis=0, keepdims=True)

        def values(c_prev, slot_prev):
            pv = jnp.dot(vt_ref[c_prev], p_scr[slot_prev], preferred_element_type=F32)
            acc_scr[...] = a_scr[slot_prev] * acc_scr[...] + pv

        def step(c, slot):
            m_prev = m_scr[...]
            m_new = jnp.maximum(m_prev, mx_scr[slot])
            p_scr[slot] = jnp.exp2(s_scr[slot] - m_new).astype(p_scr.dtype)
            a_scr[slot] = jnp.exp2(m_prev - m_new)
            m_scr[...] = m_new
            values(jnp.maximum(c - 1, 0), 1 - slot)
            scores(jnp.minimum(c + 1, n_chunks - 1), 1 - slot)

        scores(0, 0)
        run_chunks(step)
        values(n_chunks - 1, (n_chunks - 1) % 2)

    out_t = acc_scr[0:hd, :] / acc_scr[hd:hd + 1, :]
    for h in range(g):
        o_ref[:, h * hd:(h + 1) * hd] = out_t[:, h * tq:(h + 1) * tq].T.astype(o_ref.dtype)


def _attention(qk, proj, lay, dims):
    n_attn, n_kv, n_ret, gs = dims
    _, t, hd = qk.shape
    g = n_attn // n_kv
    tkv = _pick(lay.min_seq, (ATTN_KV_CHUNK, 256, 128))
    v_t = _v_transpose(proj, n_attn + n_kv, n_kv, tkv)
    out = None
    for (n_seq, seq), off in zip(lay.groups, lay.offsets):
        tq = _pick(seq, (ATTN_Q_TILE, 128))
        nq = seq // tq
        blk0 = off // seq
        n_chunks = seq // tkv

        def q_map(b, kv, i, nq=nq, off=off):
            return (kv, off // tq + b * nq + i, 0)

        def k_map(b, kv, i, blk0=blk0):
            return (n_attn + kv, blk0 + b, 0)

        def v_map(b, kv, i, blk0=blk0):
            return (kv, blk0 + b, 0, 0)

        def o_map(b, kv, i, nq=nq, off=off):
            return (off // tq + b * nq + i, kv)

        kernel = _attn_kernel
        args = [qk, qk, v_t]
        in_specs = [pl.BlockSpec((g, tq, hd), q_map),
                    pl.BlockSpec((None, seq, hd), k_map),
                    pl.BlockSpec((None, n_chunks, hd + BF16_SUBLANES, tkv), v_map)]
        aliases = {}
        if out is not None:
            args.append(out)
            in_specs.append(pl.BlockSpec(memory_space=pl.ANY))
            aliases = {3: 0}
            kernel = _attn_alias_kernel
        out = pl.pallas_call(
            kernel,
            out_shape=jax.ShapeDtypeStruct((t, n_attn * hd), BF16),
            grid=(n_seq, n_kv, nq),
            in_specs=in_specs,
            out_specs=pl.BlockSpec((tq, g * hd), o_map),
            scratch_shapes=[pltpu.VMEM((hd, g * tq), BF16), pltpu.VMEM((2, tkv, g * tq), F32),
                            pltpu.VMEM((2, 1, g * tq), F32),
                            pltpu.VMEM((2, tkv, g * tq), BF16), pltpu.VMEM((2, 1, g * tq), F32),
                            pltpu.VMEM((1, g * tq), F32),
                            pltpu.VMEM((hd + BF16_SUBLANES, g * tq), F32),
                            pltpu.VMEM((1, 1), F32)],
            input_output_aliases=aliases,
            compiler_params=_params(("parallel", "parallel", "arbitrary"),
                                    [((g, tq, hd), BF16), ((seq, hd), BF16), ((seq, hd), BF16),
                                     ((tq, g * hd), BF16)],
                                    extra=2 * _nbytes((hd, g * tq), F32) + 6 * _nbytes((tkv, g * tq), F32)),
            name="attention",
        )(*args)
    return out


def _attn_alias_kernel(q_ref, k_ref, vt_ref, prev_ref, o_ref, *scratch):
    del prev_ref
    _attn_kernel(q_ref, k_ref, vt_ref, o_ref, *scratch)


def _log_sigmoid(x):
    return -(jnp.maximum(-x, 0.0) + jnp.log1p(jnp.exp(-jnp.abs(x))))


def _ret_kernel(q_ref, k_ref, v_ref, g_ref, dec_ref, o_ref, rb_scr, rf_scr):
    seq, hd = q_ref.shape
    c = RET_CHUNK
    nc = seq // c
    lg = _log_sigmoid(dec_ref[...])
    lg_f = lg[0:1, :]
    lg_b = lg[1:2, :]
    ri = lax.broadcasted_iota(jnp.int32, (c, c), 0).astype(F32)
    ci = lax.broadcasted_iota(jnp.int32, (c, c), 1).astype(F32)
    diff = ri - ci
    decay = (jnp.where(diff >= 0, jnp.exp(jnp.where(diff >= 0, diff, 0.0) * lg_f), 0.0)
             + jnp.where(diff <= 0, jnp.exp(jnp.where(diff <= 0, -diff, 0.0) * lg_b), 0.0))
    lane = lax.broadcasted_iota(jnp.int32, (1, c), 1).astype(F32)
    zeta_f = jnp.exp((c - 1.0 - lane) * lg_f)
    zeta_b = jnp.exp(lane * lg_b)
    xi_f = jnp.exp((ri + 1.0) * lg_f)
    xi_b = jnp.exp((c - ri) * lg_b)
    g_f = jnp.exp(c * lg_f)
    g_b = jnp.exp(c * lg_b)

    nb = max(d for d in range(1, RET_BLOCK + 1) if nc % d == 0)
    rows = nb * c

    def chunk_updates(k3, v3, zeta):
        return [jnp.dot((k3[j].astype(F32).T * zeta).astype(v3.dtype), v3[j], preferred_element_type=F32)
                for j in range(nb)]

    def backward(i, r):
        b = nc // nb - 1 - i
        off = pl.multiple_of(b * rows, rows)
        us = chunk_updates(k_ref[pl.ds(off, rows), :].reshape(nb, c, hd),
                           v_ref[pl.ds(off, rows), :].reshape(nb, c, hd), zeta_b)
        for j in reversed(range(nb)):
            rb_scr[b * nb + j] = r.astype(rb_scr.dtype)
            r = g_b * r + us[j]
        return r

    lax.fori_loop(0, nc // nb, backward, jnp.zeros((hd, hd), F32))

    batched_nt = (((2,), (2,)), ((0,), (0,)))
    batched_nn = (((2,), (1,)), ((0,), (0,)))

    def forward(b, r):
        off = pl.multiple_of(b * rows, rows)
        q3 = q_ref[pl.ds(off, rows), :].reshape(nb, c, hd)
        k3 = k_ref[pl.ds(off, rows), :].reshape(nb, c, hd)
        v3 = v_ref[pl.ds(off, rows), :].reshape(nb, c, hd)
        s = lax.dot_general(q3, k3, batched_nt, preferred_element_type=F32)
        o = lax.dot_general((s * decay).astype(v3.dtype), v3, batched_nn, preferred_element_type=F32)
        us = chunk_updates(k3, v3, zeta_f)
        for j in range(nb):
            rf_scr[j] = r.astype(rf_scr.dtype)
            r = g_f * r + us[j]
        o = o + xi_f * lax.dot_general(q3, rf_scr[0:nb], batched_nn, preferred_element_type=F32)
        o = o + xi_b * lax.dot_general(q3, rb_scr[pl.ds(b * nb, nb)], batched_nn, preferred_element_type=F32)
        mu = jnp.mean(o, axis=-1, keepdims=True)
        var = jnp.mean(jnp.square(o - mu), axis=-1, keepdims=True)
        on = (o - mu) * lax.rsqrt(var + GN_EPS)
        gate = g_ref[pl.ds(off, rows), :].astype(F32).reshape(nb, c, hd)
        o_ref[pl.ds(off, rows), :] = (_silu(gate) * on).astype(o_ref.dtype).reshape(rows, hd)
        return r

    lax.fori_loop(0, nc // nb, forward, jnp.zeros((hd, hd), F32))


def _retention(qk, proj, dec, lay, dims):
    n_attn, n_kv, n_ret, gs = dims
    _, t, hd = qk.shape
    q0 = n_attn + n_kv
    k0 = q0 + n_ret
    v0 = n_attn + 2 * n_kv + 2 * n_ret
    g0 = v0 + n_ret
    out = None
    for (n_seq, seq), off in zip(lay.groups, lay.offsets):
        blk0 = off // seq

        def slab(base, blk0=blk0):
            return pl.BlockSpec((None, seq, hd), lambda b, h: (base + h, blk0 + b, 0))

        args = [qk, qk, proj, proj, dec]
        in_specs = [slab(q0), slab(k0), slab(v0), slab(g0),
                    pl.BlockSpec((None, 2, hd), lambda b, h: (h, 0, 0))]
        kernel = _ret_kernel
        aliases = {}
        if out is not None:
            args.append(out)
            in_specs.append(pl.BlockSpec(memory_space=pl.ANY))
            aliases = {5: 0}
            kernel = _ret_alias_kernel
        out = pl.pallas_call(
            kernel,
            out_shape=jax.ShapeDtypeStruct((t, n_ret * hd), BF16),
            grid=(n_seq, n_ret),
            in_specs=in_specs,
            out_specs=pl.BlockSpec((seq, hd), lambda b, h, blk0=blk0: (blk0 + b, h)),
            scratch_shapes=[pltpu.VMEM((seq // RET_CHUNK, hd, hd), BF16),
                            pltpu.VMEM((RET_BLOCK, hd, hd), BF16)],
            input_output_aliases=aliases,
            compiler_params=_params(("parallel", "parallel"), [((seq, hd), BF16)] * 5,
                                    extra=_nbytes((seq, hd), BF16)),
            name="retention",
        )(*args)
    return out


def _ret_alias_kernel(q_ref, k_ref, v_ref, g_ref, dec_ref, prev_ref, o_ref, rb_scr, rf_scr):
    del prev_ref
    _ret_kernel(q_ref, k_ref, v_ref, g_ref, dec_ref, o_ref, rb_scr, rf_scr)


def _out_proj_kernel(oa_ref, or_ref, wa_ref, wr_ref, x_ref, ga_ref, o_ref):
    acc = jnp.dot(oa_ref[...], wa_ref[...], preferred_element_type=F32)
    acc = acc + jnp.dot(or_ref[...], wr_ref[...], preferred_element_type=F32)
    o_ref[...] = x_ref[...] + ga_ref[...] * acc


def _out_proj(o_attn, o_ret, w_out, x, mod3, lay, layer, gate_chunk):
    t, ka = o_attn.shape
    kr = o_ret.shape[-1]
    assert ka == kr
    d = x.shape[-1]
    tm = _pick(lay.min_seq, (1024, 512, 256, 128))
    tn = _pick(d, (512, 256, 128))
    nj = d // tn
    tile = pl.BlockSpec((tm, tn), lambda i, j: (i, j))
    return pl.pallas_call(
        _out_proj_kernel,
        out_shape=jax.ShapeDtypeStruct((t, d), F32),
        grid=(t // tm, nj),
        in_specs=[
            pl.BlockSpec((tm, ka), lambda i, j: (i, 0)),
            pl.BlockSpec((tm, kr), lambda i, j: (i, 0)),
            pl.BlockSpec((None, ka, tn), lambda i, j: (layer, 0, j)),
            pl.BlockSpec((None, kr, tn), lambda i, j: (layer, 1, j)),
            tile,
            pl.BlockSpec((None, 1, tn), lambda i, j: (lay.seq_row(i * tm), 0, gate_chunk * nj + j)),
        ],
        out_specs=tile,
        compiler_params=_params(("parallel", "arbitrary"),
                                [((tm, ka), BF16), ((tm, kr), BF16), ((ka, tn), BF16), ((kr, tn), BF16),
                                 ((tm, tn), F32), ((tm, tn), F32)],
                                extra=2 * _nbytes((tm, tn), F32)),
        name="out_proj",
    )(o_attn, o_ret, w_out, w_out, x, mod3)


def kernel(x_prompt, x_sample, c_prompt, c_sample, w_mod, b_mod, g_ffn1, ffn1_w1, ffn1_w3, ffn1_w2, g_mix, w_in, q_norm_g, k_norm_g, ret_decay_f, ret_decay_b, w_out, g_ffn2, ffn2_w1, ffn2_w3, ffn2_w2, g_post):
    depth, d, _ = w_mod.shape
    hd = HEAD_DIM
    n_attn = d // (2 * hd)
    n_kv = n_attn // GQA_RATIO
    n_ret = d // (2 * hd)
    gs = 4 if n_kv % 4 == 0 else 1
    dims = (n_attn, n_kv, n_ret, gs)

    named = [(x_prompt, c_prompt), (x_sample, c_sample)]
    order = sorted(range(2), key=lambda i: -named[i][0].shape[1])
    xs = [named[i][0] for i in order]
    cs = [named[i][1] for i in order]
    lay = _Layout([(xg.shape[0], xg.shape[1]) for xg in xs])
    x = jnp.concatenate([xg.reshape(-1, d) for xg in xs], axis=0)
    c_rows = jnp.concatenate(cs + [jnp.zeros((MOD_ROWS - lay.n_rows, d), F32)], axis=0)

    mod = _modulation(c_rows, w_mod, b_mod)
    cos2, sin2 = _rope_tables(lay.max_seq)

    bf = lambda w: w.astype(BF16)
    w1a, w3a, w2a = bf(ffn1_w1), bf(ffn1_w3), bf(ffn1_w2)
    w1b, w3b, w2b = bf(ffn2_w1), bf(ffn2_w3), bf(ffn2_w2)
    w_in_b, w_out_b = bf(w_in), bf(w_out)

    scale = hd ** -0.5
    ones = jnp.ones((hd,), F32)
    h = None
    for l in range(depth):
        mod3 = mod[l].reshape(MOD_ROWS, 1, N_MOD * d)
        gains = ([q_norm_g[l] * (scale * LOG2_E)] * (n_attn // gs) + [k_norm_g[l]] * (n_kv // gs)
                 + [ones] * (n_ret // gs) + [ones * scale] * (n_ret // gs))
        flags = [ones] * ((n_attn + n_kv) // gs) + [0.0 * ones] * (2 * n_ret // gs)
        gain = jnp.stack(gains).reshape(-1, 1, hd)
        flag = jnp.stack(flags).reshape(-1, 1, hd)
        dec = jnp.broadcast_to(jnp.stack([ret_decay_f[l], ret_decay_b[l]], axis=1)[:, :, None],
                               (n_ret, 2, hd)).astype(F32)

        if l == 0:
            h = _norm_mod(x, g_ffn1[l], mod3, lay, 0, 1)
        u = _ffn_up(h, w1a, w3a, l)
        x = _ffn_down(u, w2a, x, mod3, lay, l, 2)

        h = _norm_mod(x, g_mix[l], mod3, lay, 3, 4)
        proj = _in_proj(h, w_in_b, l)
        qk = _qk_prep(proj, cos2, sin2, gain, flag, lay, dims)
        o_attn = _attention(qk, proj, lay, dims)
        o_ret = _retention(qk, proj, dec, lay, dims)
        x = _out_proj(o_attn, o_ret, w_out_b, x, mod3, lay, l, 5)

        h = _norm_mod(x, g_ffn2[l], mod3, lay, 6, 7)
        u = _ffn_up(h, w1b, w3b, l)
        x = _ffn_down(u, w2b, x, mod3, lay, l, 8)

        if l + 1 < depth:
            mod3n = mod[l + 1].reshape(MOD_ROWS, 1, N_MOD * d)
            x, h = _norm_mod(x, g_ffn1[l + 1], mod3n, lay, 0, 1, g_post=g_post[l])

    outs = [None, None]
    for slot, (xg, off) in enumerate(zip(xs, lay.offsets)):
        n_tok = xg.shape[0] * xg.shape[1]
        outs[order[slot]] = _post_norm(x, g_post[depth - 1], off, n_tok).reshape(xg.shape)
    return tuple(outs)
```

```python
import functools

import jax
import jax.numpy as jnp
from jax import lax
from jax.experimental import pallas as pl
from jax.experimental.pallas import tpu as pltpu

HEAD_DIM = 128
GQA_RATIO = 4
GRID_W = 64
RET_CHUNK = 128
ROPE_THETA = 10000.0
NORM_EPS = 1e-6
GN_EPS = 1e-5
N_MOD = 9
MOD_ROWS = 8
LOG2_E = 1.4426950408889634
BF16_SUBLANES = 16

ATTN_Q_TILE = 512
ATTN_KV_CHUNK = 512
ATTN_UNROLL = 2
ATTN_FAST_BOUND = 40.0
RET_BLOCK = 16

V7X_VMEM_REQUEST_CAP = 60 * 1024 * 1024
VMEM_REQUEST_FLOOR = 16 * 1024 * 1024
CAST_BLOCK_BYTES = 12 * 1024 * 1024

F32 = jnp.float32
BF16 = jnp.bfloat16


def _pick(n, prefs):
    for p in prefs:
        if n % p == 0:
            return p
    return n


def _nbytes(shape, dtype):
    size = 1
    for s in shape:
        size *= s
    return size * jnp.dtype(dtype).itemsize


def _params(semantics, blocks, extra=0):
    need = 2 * sum(_nbytes(s, d) for s, d in blocks) + extra + (4 << 20)
    need = max(VMEM_REQUEST_FLOOR, min(V7X_VMEM_REQUEST_CAP, need))
    return pltpu.CompilerParams(dimension_semantics=semantics, vmem_limit_bytes=int(need))


def _silu(x):
    return x * jax.nn.sigmoid(x)


def _rms(x, eps):
    return x * lax.rsqrt(jnp.mean(x * x, axis=-1, keepdims=True) + eps)


class _Layout:
    def __init__(self, groups):
        self.groups = groups
        self.offsets = []
        self.row0 = []
        off = 0
        row = 0
        for n_seq, seq_len in groups:
            assert off % seq_len == 0, "group offset must be a multiple of its sequence length"
            self.offsets.append(off)
            self.row0.append(row)
            off += n_seq * seq_len
            row += n_seq
        self.total = off
        self.n_rows = row
        self.min_seq = min(s for _, s in groups)
        self.max_seq = max(s for _, s in groups)

    def seq_row(self, t0):
        row = None
        for idx in reversed(range(len(self.groups))):
            here = self.row0[idx] + (t0 - self.offsets[idx]) // self.groups[idx][1]
            row = here if row is None else jnp.where(t0 < self.offsets[idx + 1], here, row)
        return row

    def pos(self, t0):
        p = None
        for idx in reversed(range(len(self.groups))):
            here = (t0 - self.offsets[idx]) % self.groups[idx][1]
            p = here if p is None else jnp.where(t0 < self.offsets[idx + 1], here, p)
        return p


def _mod_kernel(c_ref, w_ref, b_ref, o_ref):
    s = _silu(c_ref[...]).astype(BF16)
    o_ref[...] = jnp.dot(s, w_ref[...].astype(BF16), preferred_element_type=F32) + b_ref[...]


def _cast_kernel(w_ref, o_ref):
    o_ref[...] = w_ref[...].astype(o_ref.dtype)


def _to_bf16(w):
    depth, rows, cols = w.shape
    tr = next((p for p in (1024, 512, 256, 128) if rows % p == 0 and p * cols * 4 <= CAST_BLOCK_BYTES), rows)
    spec = pl.BlockSpec((None, tr, cols), lambda l, i: (l, i, 0))
    return pl.pallas_call(
        _cast_kernel,
        out_shape=jax.ShapeDtypeStruct(w.shape, BF16),
        grid=(depth, rows // tr),
        in_specs=[spec],
        out_specs=spec,
        compiler_params=_params(("parallel", "parallel"), [((tr, cols), F32), ((tr, cols), BF16)]),
        name="to_bf16",
    )(w)


def _modulation(c_rows, w_mod, b_mod):
    depth, d, n = w_mod.shape
    tn = _pick(n, (512, 256, 128))
    return pl.pallas_call(
        _mod_kernel,
        out_shape=jax.ShapeDtypeStruct((depth, MOD_ROWS, n), F32),
        grid=(depth, n // tn),
        in_specs=[
            pl.BlockSpec((MOD_ROWS, d), lambda l, j: (0, 0)),
            pl.BlockSpec((None, d, tn), lambda l, j: (l, 0, j)),
            pl.BlockSpec((None, 1, tn), lambda l, j: (l, 0, j)),
        ],
        out_specs=pl.BlockSpec((None, MOD_ROWS, tn), lambda l, j: (l, 0, j)),
        compiler_params=_params(("parallel", "parallel"), [((d, tn), F32), ((MOD_ROWS, d), F32)],
                                extra=_nbytes((d, tn), BF16)),
        name="adaln_mod",
    )(c_rows, w_mod, b_mod.reshape(depth, 1, n))


def _norm_mod_kernel(x_ref, g_ref, sc_ref, sh_ref, h_ref):
    y = _rms(x_ref[...], NORM_EPS) * g_ref[...]
    h_ref[...] = (y * (1.0 + sc_ref[...]) + sh_ref[...]).astype(h_ref.dtype)


def _post_norm_mod_kernel(x_ref, gp_ref, g_ref, sc_ref, sh_ref, xo_ref, h_ref):
    xo = _rms(x_ref[...], NORM_EPS) * gp_ref[...]
    xo_ref[...] = xo
    y = _rms(xo, NORM_EPS) * g_ref[...]
    h_ref[...] = (y * (1.0 + sc_ref[...]) + sh_ref[...]).astype(h_ref.dtype)


def _post_norm_kernel(x_ref, gp_ref, xo_ref):
    xo_ref[...] = _rms(x_ref[...], NORM_EPS) * gp_ref[...]


def _mod_spec(lay, tm, d, chunk):
    return pl.BlockSpec((None, 1, d), lambda i: (lay.seq_row(i * tm), 0, chunk))


def _norm_mod(x, g, mod3, lay, shift_chunk, scale_chunk, g_post=None):
    t, d = x.shape
    tm = _pick(lay.min_seq, (512, 256, 128))
    row = pl.BlockSpec((tm, d), lambda i: (i, 0))
    vec = pl.BlockSpec((1, d), lambda i: (0, 0))
    blocks = [((tm, d), F32), ((tm, d), BF16)]
    if g_post is None:
        return pl.pallas_call(
            _norm_mod_kernel,
            out_shape=jax.ShapeDtypeStruct((t, d), BF16),
            grid=(t // tm,),
            in_specs=[row, vec, _mod_spec(lay, tm, d, scale_chunk), _mod_spec(lay, tm, d, shift_chunk)],
            out_specs=row,
            compiler_params=_params(("parallel",), blocks, extra=2 * _nbytes((tm, d), F32)),
            name="norm_mod",
        )(x, g.reshape(1, d), mod3, mod3)
    return pl.pallas_call(
        _post_norm_mod_kernel,
        out_shape=(jax.ShapeDtypeStruct((t, d), F32), jax.ShapeDtypeStruct((t, d), BF16)),
        grid=(t // tm,),
        in_specs=[row, vec, vec, _mod_spec(lay, tm, d, scale_chunk), _mod_spec(lay, tm, d, shift_chunk)],
        out_specs=(row, row),
        compiler_params=_params(("parallel",), blocks + [((tm, d), F32)], extra=2 * _nbytes((tm, d), F32)),
        name="post_norm_mod",
    )(x, g_post.reshape(1, d), g.reshape(1, d), mod3, mod3)


def _post_norm(x, g_post, t0, n_tok):
    _, d = x.shape
    tm = _pick(n_tok, (512, 256, 128))
    assert t0 % tm == 0
    return pl.pallas_call(
        _post_norm_kernel,
        out_shape=jax.ShapeDtypeStruct((n_tok, d), F32),
        grid=(n_tok // tm,),
        in_specs=[pl.BlockSpec((tm, d), lambda i: (t0 // tm + i, 0)), pl.BlockSpec((1, d), lambda i: (0, 0))],
        out_specs=pl.BlockSpec((tm, d), lambda i: (i, 0)),
        compiler_params=_params(("parallel",), [((tm, d), F32)] * 2, extra=_nbytes((tm, d), F32)),
        name="post_norm",
    )(x, g_post.reshape(1, d))


def _ffn_up_kernel(h_ref, w1_ref, w3_ref, u_ref):
    h = h_ref[...]
    a = jnp.dot(h, w1_ref[...], preferred_element_type=F32)
    b = jnp.dot(h, w3_ref[...], preferred_element_type=F32)
    u_ref[...] = (_silu(a) * b).astype(u_ref.dtype)


def _ffn_up(h, w1, w3, layer):
    t, d = h.shape
    f = w1.shape[-1]
    tm = _pick(t, (1024, 512, 256, 128))
    tn = _pick(f, (512, 256, 128))
    wspec = pl.BlockSpec((None, d, tn), lambda i, j: (layer, 0, j))
    return pl.pallas_call(
        _ffn_up_kernel,
        out_shape=jax.ShapeDtypeStruct((t, f), BF16),
        grid=(t // tm, f // tn),
        in_specs=[pl.BlockSpec((tm, d), lambda i, j: (i, 0)), wspec, wspec],
        out_specs=pl.BlockSpec((tm, tn), lambda i, j: (i, j)),
        compiler_params=_params(("parallel", "arbitrary"),
                                [((tm, d), BF16), ((d, tn), BF16), ((d, tn), BF16), ((tm, tn), BF16)],
                                extra=3 * _nbytes((tm, tn), F32)),
        name="ffn_up",
    )(h, w1, w3)


def _ffn_down_kernel(u_ref, w2_ref, x_ref, ga_ref, o_ref):
    acc = jnp.dot(u_ref[...], w2_ref[...], preferred_element_type=F32)
    o_ref[...] = x_ref[...] + (0.5 * ga_ref[...]) * acc


def _ffn_down(u, w2, x, mod3, lay, layer, gate_chunk):
    t, f = u.shape
    d = x.shape[-1]
    tm = _pick(lay.min_seq, (512, 256, 128))
    tn = _pick(d, (512, 256, 128))
    nj = d // tn
    tile = pl.BlockSpec((tm, tn), lambda i, j: (i, j))
    return pl.pallas_call(
        _ffn_down_kernel,
        out_shape=jax.ShapeDtypeStruct((t, d), F32),
        grid=(t // tm, nj),
        in_specs=[
            pl.BlockSpec((tm, f), lambda i, j: (i, 0)),
            pl.BlockSpec((None, f, tn), lambda i, j: (layer, 0, j)),
            tile,
            pl.BlockSpec((None, 1, tn), lambda i, j: (lay.seq_row(i * tm), 0, gate_chunk * nj + j)),
        ],
        out_specs=tile,
        compiler_params=_params(("parallel", "arbitrary"),
                                [((tm, f), BF16), ((f, tn), BF16), ((tm, tn), F32), ((tm, tn), F32)],
                                extra=2 * _nbytes((tm, tn), F32)),
        name="ffn_down",
    )(u, w2, x, mod3)


def _in_proj_kernel(h_ref, w_ref, o_ref):
    acc = jnp.dot(h_ref[...], w_ref[...], preferred_element_type=F32)
    for c in range(o_ref.shape[0]):
        o_ref[c] = acc[:, c * HEAD_DIM:(c + 1) * HEAD_DIM].astype(o_ref.dtype)


def _in_proj(h, w_in, layer):
    t, d = h.shape
    n = w_in.shape[-1]
    tm = _pick(t, (1024, 512, 256, 128))
    tn = _pick(n, (1024, 512, 256, 128))
    sub = tn // HEAD_DIM
    return pl.pallas_call(
        _in_proj_kernel,
        out_shape=jax.ShapeDtypeStruct((n // HEAD_DIM, t, HEAD_DIM), BF16),
        grid=(t // tm, n // tn),
        in_specs=[pl.BlockSpec((tm, d), lambda i, j: (i, 0)),
                  pl.BlockSpec((None, d, tn), lambda i, j: (layer, 0, j))],
        out_specs=pl.BlockSpec((sub, tm, HEAD_DIM), lambda i, j: (j, i, 0)),
        compiler_params=_params(("parallel", "arbitrary"),
                                [((tm, d), BF16), ((d, tn), BF16), ((tm, tn), BF16)],
                                extra=2 * _nbytes((tm, tn), F32)),
        name="in_proj",
    )(h, w_in)


def _rope_tables(n_pos):
    half = HEAD_DIM // 2
    p = jnp.arange(n_pos, dtype=jnp.int32)
    row = (p // GRID_W).astype(F32)
    col = (p % GRID_W).astype(F32)
    inv = ROPE_THETA ** (-jnp.arange(0, half, 2, dtype=F32) / half)
    ang = jnp.concatenate([row[:, None] * inv, col[:, None] * inv], axis=-1)
    cos = jnp.cos(ang)
    sin = jnp.sin(ang)
    cos2 = jnp.repeat(cos, 2, axis=-1)
    sin2 = jnp.stack([-sin, sin], axis=-1).reshape(n_pos, HEAD_DIM)
    return cos2, sin2


def _qk_prep_kernel(p_ref, cos_ref, sin_ref, gain_ref, o_ref, *, n_norm_groups):
    gs, tt, hd = p_ref.shape
    x = p_ref[...].astype(F32).reshape(gs * tt, hd)

    def rotate_and_store(y):
        lane = lax.broadcasted_iota(jnp.int32, y.shape, 1)
        partner = jnp.where(lane % 2 == 0, pltpu.roll(y, hd - 1, axis=1), pltpu.roll(y, 1, axis=1))
        y = y.reshape(gs, tt, hd)
        partner = partner.reshape(gs, tt, hd)
        o_ref[...] = (y * cos_ref[...] + partner * sin_ref[...]).astype(o_ref.dtype)

    @pl.when(pl.program_id(0) < n_norm_groups)
    def _():
        rotate_and_store(_rms(x, NORM_EPS) * gain_ref[...])

    @pl.when(pl.program_id(0) >= n_norm_groups)
    def _():
        rotate_and_store(x * gain_ref[...])


def _qk_prep(proj, cos2, sin2, gain, lay, dims):
    n_attn, n_kv, n_ret, gs = dims
    _, t, hd = proj.shape
    tt = _pick(lay.min_seq, (1024, 512, 256, 128))
    n_groups = (n_attn + n_kv + 2 * n_ret) // gs
    skip_at = (n_attn + n_kv) // gs
    skip = n_kv // gs

    def src(g, i):
        return (jnp.where(g >= skip_at, g + skip, g), i, 0)

    tab = pl.BlockSpec((tt, hd), lambda g, i: (lay.pos(i * tt) // tt, 0))
    par = pl.BlockSpec((None, 1, hd), lambda g, i: (g, 0, 0))
    return pl.pallas_call(
        functools.partial(_qk_prep_kernel, n_norm_groups=(n_attn + n_kv) // gs),
        out_shape=jax.ShapeDtypeStruct((n_groups * gs, t, hd), BF16),
        grid=(n_groups, t // tt),
        in_specs=[pl.BlockSpec((gs, tt, hd), src), tab, tab, par],
        out_specs=pl.BlockSpec((gs, tt, hd), lambda g, i: (g, i, 0)),
        compiler_params=_params(("parallel", "parallel"),
                                [((gs, tt, hd), BF16)] * 2 + [((tt, hd), F32)] * 2,
                                extra=6 * _nbytes((gs, tt, hd), F32)),
        name="qk_prep",
    )(proj, cos2, sin2, gain)


def _v_transpose_kernel(v_ref, o_ref):
    tkv, hd = v_ref.shape
    o_ref[0:hd, :] = v_ref[...].astype(F32).T.astype(o_ref.dtype)
    o_ref[hd:, :] = jnp.ones((o_ref.shape[0] - hd, tkv), o_ref.dtype)


def _v_transpose(proj, slot0, n_slots, tkv):
    _, t, hd = proj.shape
    rows = hd + BF16_SUBLANES
    return pl.pallas_call(
        _v_transpose_kernel,
        out_shape=jax.ShapeDtypeStruct((n_slots, t // tkv, rows, tkv), proj.dtype),
        grid=(n_slots, t // tkv),
        in_specs=[pl.BlockSpec((None, tkv, hd), lambda s, i: (slot0 + s, i, 0))],
        out_specs=pl.BlockSpec((None, None, rows, tkv), lambda s, i: (s, i, 0, 0)),
        compiler_params=_params(("parallel", "parallel"), [((tkv, hd), BF16), ((rows, tkv), BF16)],
                                extra=2 * _nbytes((tkv, hd), F32)),
        name="v_transpose",
    )(proj)


def _attn_kernel(q_ref, k_ref, vt_ref, o_ref, qt_scr, s_scr, mx_scr, p_scr, a_scr, m_scr, acc_scr, kmax_scr):
    g, tq, hd = q_ref.shape
    n_chunks, _, tkv = vt_ref.shape
    nq = g * tq
    qt_scr[...] = q_ref[...].reshape(nq, hd).astype(F32).T.astype(qt_scr.dtype)
    acc_scr[...] = jnp.zeros(acc_scr.shape, F32)
    p_scr[1] = jnp.zeros(p_scr.shape[1:], p_scr.dtype)

    def key_chunk(c):
        return k_ref[pl.ds(pl.multiple_of(c * tkv, tkv), tkv), :]

    @pl.when(pl.program_id(2) == 0)
    def _():
        def chunk_max(c, mx):
            kf = key_chunk(c).astype(F32)
            return jnp.maximum(mx, jnp.max(jnp.sum(kf * kf, axis=1, keepdims=True), axis=0, keepdims=True))

        kmax_scr[...] = lax.fori_loop(0, n_chunks, chunk_max, jnp.zeros((1, 1), F32))

    qf = qt_scr[...].astype(F32)
    bound = jnp.sqrt(jnp.sum(qf * qf, axis=0, keepdims=True) * kmax_scr[...])
    small = jnp.max(bound) < ATTN_FAST_BOUND

    def run_chunks(step):
        unroll = ATTN_UNROLL if n_chunks % ATTN_UNROLL == 0 else 2

        def body(i, carry):
            for u in range(unroll):
                step(unroll * i + u, u % 2)
            return carry

        lax.fori_loop(0, n_chunks // unroll, body, 0)
        if n_chunks % unroll:
            step(jnp.int32(n_chunks - 1), 0)

    @pl.when(small)
    def _():
        def step(c, slot):
            s = jnp.dot(key_chunk(c), qt_scr[...], preferred_element_type=F32)
            p_scr[slot] = jnp.exp2(s - bound).astype(p_scr.dtype)
            acc_scr[...] += jnp.dot(vt_ref[jnp.maximum(c - 1, 0)], p_scr[1 - slot],
                                    preferred_element_type=F32)

        run_chunks(step)
        last = (n_chunks - 1) % 2
        acc_scr[...] += jnp.dot(vt_ref[n_chunks - 1], p_scr[last], preferred_element_type=F32)

    @pl.when(jnp.logical_not(small))
    def _():
        m_scr[...] = jnp.full(m_scr.shape, -jnp.inf, F32)
        a_scr[1] = jnp.ones(a_scr.shape[1:], F32)

        def scores(c, slot):
            s = jnp.dot(key_chunk(c), qt_scr[...], preferred_element_type=F32)
            s_scr[slot] = s
            mx_scr[slot] = jnp.max(s, axis=0, keepdims=True)

        def values(c_prev, slot_prev):
            pv = jnp.dot(vt_ref[c_prev], p_scr[slot_prev], preferred_element_type=F32)
            acc_scr[...] = a_scr[slot_prev] * acc_scr[...] + pv

        def step(c, slot):
            m_prev = m_scr[...]
            m_new = jnp.maximum(m_prev, mx_scr[slot])
            p_scr[slot] = jnp.exp2(s_scr[slot] - m_new).astype(p_scr.dtype)
            a_scr[slot] = jnp.exp2(m_prev - m_new)
            m_scr[...] = m_new
            values(jnp.maximum(c - 1, 0), 1 - slot)
            scores(jnp.minimum(c + 1, n_chunks - 1), 1 - slot)

        scores(0, 0)
        run_chunks(step)
        values(n_chunks - 1, (n_chunks - 1) % 2)

    out_t = acc_scr[0:hd, :] / acc_scr[hd:hd + 1, :]
    for h in range(g):
        o_ref[:, h * hd:(h + 1) * hd] = out_t[:, h * tq:(h + 1) * tq].T.astype(o_ref.dtype)


def _attention(qk, proj, lay, dims):
    n_attn, n_kv, n_ret, gs = dims
    _, t, hd = qk.shape
    g = n_attn // n_kv
    tkv = _pick(lay.min_seq, (ATTN_KV_CHUNK, 256, 128))
    v_t = _v_transpose(proj, n_attn + n_kv, n_kv, tkv)
    out = None
    for (n_seq, seq), off in zip(lay.groups, lay.offsets):
        tq = _pick(seq, (ATTN_Q_TILE, 128))
        nq = seq // tq
        blk0 = off // seq
        n_chunks = seq // tkv

        def q_map(b, kv, i, nq=nq, off=off):
            return (kv, off // tq + b * nq + i, 0)

        def k_map(b, kv, i, blk0=blk0):
            return (n_attn + kv, blk0 + b, 0)

        def v_map(b, kv, i, blk0=blk0):
            return (kv, blk0 + b, 0, 0)

        def o_map(b, kv, i, nq=nq, off=off):
            return (off // tq + b * nq + i, kv)

        kernel = _attn_kernel
        args = [qk, qk, v_t]
        in_specs = [pl.BlockSpec((g, tq, hd), q_map),
                    pl.BlockSpec((None, seq, hd), k_map),
                    pl.BlockSpec((None, n_chunks, hd + BF16_SUBLANES, tkv), v_map)]
        aliases = {}
        if out is not None:
            args.append(out)
            in_specs.append(pl.BlockSpec(memory_space=pl.ANY))
            aliases = {3: 0}
            kernel = _attn_alias_kernel
        out = pl.pallas_call(
            kernel,
            out_shape=jax.ShapeDtypeStruct((t, n_attn * hd), BF16),
            grid=(n_seq, n_kv, nq),
            in_specs=in_specs,
            out_specs=pl.BlockSpec((tq, g * hd), o_map),
            scratch_shapes=[pltpu.VMEM((hd, g * tq), BF16), pltpu.VMEM((2, tkv, g * tq), F32),
                            pltpu.VMEM((2, 1, g * tq), F32),
                            pltpu.VMEM((2, tkv, g * tq), BF16), pltpu.VMEM((2, 1, g * tq), F32),
                            pltpu.VMEM((1, g * tq), F32),
                            pltpu.VMEM((hd + BF16_SUBLANES, g * tq), F32),
                            pltpu.VMEM((1, 1), F32)],
            input_output_aliases=aliases,
            compiler_params=_params(("parallel", "parallel", "arbitrary"),
                                    [((g, tq, hd), BF16), ((seq, hd), BF16), ((seq, hd), BF16),
                                     ((tq, g * hd), BF16)],
                                    extra=2 * _nbytes((hd, g * tq), F32) + 6 * _nbytes((tkv, g * tq), F32)),
            name="attention",
        )(*args)
    return out


def _attn_alias_kernel(q_ref, k_ref, vt_ref, prev_ref, o_ref, *scratch):
    del prev_ref
    _attn_kernel(q_ref, k_ref, vt_ref, o_ref, *scratch)


def _log_sigmoid(x):
    return -(jnp.maximum(-x, 0.0) + jnp.log1p(jnp.exp(-jnp.abs(x))))


def _ret_kernel(q_ref, k_ref, v_ref, g_ref, dec_ref, o_ref, rb_scr, rf_scr):
    seq, hd = q_ref.shape
    c = RET_CHUNK
    nc = seq // c
    lg = _log_sigmoid(dec_ref[...])
    lg_f = lg[0:1, :]
    lg_b = lg[1:2, :]
    ri = lax.broadcasted_iota(jnp.int32, (c, c), 0).astype(F32)
    ci = lax.broadcasted_iota(jnp.int32, (c, c), 1).astype(F32)
    diff = ri - ci
    decay = (jnp.where(diff >= 0, jnp.exp(jnp.where(diff >= 0, diff, 0.0) * lg_f), 0.0)
             + jnp.where(diff <= 0, jnp.exp(jnp.where(diff <= 0, -diff, 0.0) * lg_b), 0.0))
    lane = lax.broadcasted_iota(jnp.int32, (1, c), 1).astype(F32)
    zeta_f = jnp.exp((c - 1.0 - lane) * lg_f)
    zeta_b = jnp.exp(lane * lg_b)
    xi_f = jnp.exp((ri + 1.0) * lg_f)
    xi_b = jnp.exp((c - ri) * lg_b)
    g_f = jnp.exp(c * lg_f)
    g_b = jnp.exp(c * lg_b)

    nb = max(d for d in range(1, RET_BLOCK + 1) if nc % d == 0)
    rows = nb * c

    def chunk_updates(k3, v3, zeta):
        return [jnp.dot((k3[j].astype(F32).T * zeta).astype(v3.dtype), v3[j], preferred_element_type=F32)
                for j in range(nb)]

    def backward(i, r):
        b = nc // nb - 1 - i
        off = pl.multiple_of(b * rows, rows)
        us = chunk_updates(k_ref[pl.ds(off, rows), :].reshape(nb, c, hd),
                           v_ref[pl.ds(off, rows), :].reshape(nb, c, hd), zeta_b)
        for j in reversed(range(nb)):
            rb_scr[b * nb + j] = r.astype(rb_scr.dtype)
            r = g_b * r + us[j]
        return r

    lax.fori_loop(0, nc // nb, backward, jnp.zeros((hd, hd), F32))

    batched_nt = (((2,), (2,)), ((0,), (0,)))
    batched_nn = (((2,), (1,)), ((0,), (0,)))

    def forward(b, r):
        off = pl.multiple_of(b * rows, rows)
        q3 = q_ref[pl.ds(off, rows), :].reshape(nb, c, hd)
        k3 = k_ref[pl.ds(off, rows), :].reshape(nb, c, hd)
        v3 = v_ref[pl.ds(off, rows), :].reshape(nb, c, hd)
        s = lax.dot_general(q3, k3, batched_nt, preferred_element_type=F32)
        o = lax.dot_general((s * decay).astype(v3.dtype), v3, batched_nn, preferred_element_type=F32)
        us = chunk_updates(k3, v3, zeta_f)
        for j in range(nb):
            rf_scr[j] = r.astype(rf_scr.dtype)
            r = g_f * r + us[j]
        o = o + xi_f * lax.dot_general(q3, rf_scr[0:nb], batched_nn, preferred_element_type=F32)
        o = o + xi_b * lax.dot_general(q3, rb_scr[pl.ds(b * nb, nb)], batched_nn, preferred_element_type=F32)
        mu = jnp.mean(o, axis=-1, keepdims=True)
        var = jnp.mean(jnp.square(o - mu), axis=-1, keepdims=True)
        on = (o - mu) * lax.rsqrt(var + GN_EPS)
        gate = g_ref[pl.ds(off, rows), :].astype(F32).reshape(nb, c, hd)
        o_ref[pl.ds(off, rows), :] = (_silu(gate) * on).astype(o_ref.dtype).reshape(rows, hd)
        return r

    lax.fori_loop(0, nc // nb, forward, jnp.zeros((hd, hd), F32))


def _retention(qk, proj, dec, lay, dims):
    n_attn, n_kv, n_ret, gs = dims
    _, t, hd = qk.shape
    q0 = n_attn + n_kv
    k0 = q0 + n_ret
    v0 = n_attn + 2 * n_kv + 2 * n_ret
    g0 = v0 + n_ret
    out = None
    for (n_seq, seq), off in zip(lay.groups, lay.offsets):
        blk0 = off // seq

        def slab(base, blk0=blk0):
            return pl.BlockSpec((None, seq, hd), lambda b, h: (base + h, blk0 + b, 0))

        args = [qk, qk, proj, proj, dec]
        in_specs = [slab(q0), slab(k0), slab(v0), slab(g0),
                    pl.BlockSpec((None, 2, hd), lambda b, h: (h, 0, 0))]
        kernel = _ret_kernel
        aliases = {}
        if out is not None:
            args.append(out)
            in_specs.append(pl.BlockSpec(memory_space=pl.ANY))
            aliases = {5: 0}
            kernel = _ret_alias_kernel
        out = pl.pallas_call(
            kernel,
            out_shape=jax.ShapeDtypeStruct((t, n_ret * hd), BF16),
            grid=(n_seq, n_ret),
            in_specs=in_specs,
            out_specs=pl.BlockSpec((seq, hd), lambda b, h, blk0=blk0: (blk0 + b, h)),
            scratch_shapes=[pltpu.VMEM((seq // RET_CHUNK, hd, hd), BF16),
                            pltpu.VMEM((RET_BLOCK, hd, hd), BF16)],
            input_output_aliases=aliases,
            compiler_params=_params(("parallel", "parallel"), [((seq, hd), BF16)] * 5,
                                    extra=_nbytes((seq, hd), BF16)),
            name="retention",
        )(*args)
    return out


def _ret_alias_kernel(q_ref, k_ref, v_ref, g_ref, dec_ref, prev_ref, o_ref, rb_scr, rf_scr):
    del prev_ref
    _ret_kernel(q_ref, k_ref, v_ref, g_ref, dec_ref, o_ref, rb_scr, rf_scr)


def _out_proj_kernel(oa_ref, or_ref, wa_ref, wr_ref, x_ref, ga_ref, o_ref):
    acc = jnp.dot(oa_ref[...], wa_ref[...], preferred_element_type=F32)
    acc = acc + jnp.dot(or_ref[...], wr_ref[...], preferred_element_type=F32)
    o_ref[...] = x_ref[...] + ga_ref[...] * acc


def _out_proj(o_attn, o_ret, w_out, x, mod3, lay, layer, gate_chunk):
    t, ka = o_attn.shape
    kr = o_ret.shape[-1]
    assert ka == kr
    d = x.shape[-1]
    tm = _pick(lay.min_seq, (1024, 512, 256, 128))
    tn = _pick(d, (512, 256, 128))
    nj = d // tn
    tile = pl.BlockSpec((tm, tn), lambda i, j: (i, j))
    return pl.pallas_call(
        _out_proj_kernel,
        out_shape=jax.ShapeDtypeStruct((t, d), F32),
        grid=(t // tm, nj),
        in_specs=[
            pl.BlockSpec((tm, ka), lambda i, j: (i, 0)),
            pl.BlockSpec((tm, kr), lambda i, j: (i, 0)),
            pl.BlockSpec((None, ka, tn), lambda i, j: (layer, 0, j)),
            pl.BlockSpec((None, kr, tn), lambda i, j: (layer, 1, j)),
            tile,
            pl.BlockSpec((None, 1, tn), lambda i, j: (lay.seq_row(i * tm), 0, gate_chunk * nj + j)),
        ],
        out_specs=tile,
        compiler_params=_params(("parallel", "arbitrary"),
                                [((tm, ka), BF16), ((tm, kr), BF16), ((ka, tn), BF16), ((kr, tn), BF16),
                                 ((tm, tn), F32), ((tm, tn), F32)],
                                extra=2 * _nbytes((tm, tn), F32)),
        name="out_proj",
    )(o_attn, o_ret, w_out, w_out, x, mod3)


def kernel(x_prompt, x_sample, c_prompt, c_sample, w_mod, b_mod, g_ffn1, ffn1_w1, ffn1_w3, ffn1_w2, g_mix, w_in, q_norm_g, k_norm_g, ret_decay_f, ret_decay_b, w_out, g_ffn2, ffn2_w1, ffn2_w3, ffn2_w2, g_post):
    depth, d, _ = w_mod.shape
    hd = HEAD_DIM
    n_attn = d // (2 * hd)
    n_kv = n_attn // GQA_RATIO
    n_ret = d // (2 * hd)
    gs = 4 if n_kv % 4 == 0 else 1
    dims = (n_attn, n_kv, n_ret, gs)

    named = [(x_prompt, c_prompt), (x_sample, c_sample)]
    order = sorted(range(2), key=lambda i: -named[i][0].shape[1])
    xs = [named[i][0] for i in order]
    cs = [named[i][1] for i in order]
    lay = _Layout([(xg.shape[0], xg.shape[1]) for xg in xs])
    x = jnp.concatenate([xg.reshape(-1, d) for xg in xs], axis=0)
    c_rows = jnp.concatenate(cs + [jnp.zeros((MOD_ROWS - lay.n_rows, d), F32)], axis=0)

    mod = _modulation(c_rows, w_mod, b_mod)
    cos2, sin2 = _rope_tables(lay.max_seq)

    bf = _to_bf16
    w1a, w3a, w2a = bf(ffn1_w1), bf(ffn1_w3), bf(ffn1_w2)
    w1b, w3b, w2b = bf(ffn2_w1), bf(ffn2_w3), bf(ffn2_w2)
    w_in_b, w_out_b = bf(w_in), bf(w_out)

    scale = hd ** -0.5
    ones = jnp.ones((hd,), F32)
    h = None
    for l in range(depth):
        mod3 = mod[l].reshape(MOD_ROWS, 1, N_MOD * d)
        gains = ([q_norm_g[l] * (scale * LOG2_E)] * (n_attn // gs) + [k_norm_g[l]] * (n_kv // gs)
                 + [ones] * (n_ret // gs) + [ones * scale] * (n_ret // gs))
        gain = jnp.stack(gains).reshape(-1, 1, hd)
        dec = jnp.broadcast_to(jnp.stack([ret_decay_f[l], ret_decay_b[l]], axis=1)[:, :, None],
                               (n_ret, 2, hd)).astype(F32)

        if l == 0:
            h = _norm_mod(x, g_ffn1[l], mod3, lay, 0, 1)
        u = _ffn_up(h, w1a, w3a, l)
        x = _ffn_down(u, w2a, x, mod3, lay, l, 2)

        h = _norm_mod(x, g_mix[l], mod3, lay, 3, 4)
        proj = _in_proj(h, w_in_b, l)
        qk = _qk_prep(proj, cos2, sin2, gain, lay, dims)
        o_attn = _attention(qk, proj, lay, dims)
        o_ret = _retention(qk, proj, dec, lay, dims)
        x = _out_proj(o_attn, o_ret, w_out_b, x, mod3, lay, l, 5)

        h = _norm_mod(x, g_ffn2[l], mod3, lay, 6, 7)
        u = _ffn_up(h, w1b, w3b, l)
        x = _ffn_down(u, w2b, x, mod3, lay, l, 8)

        if l + 1 < depth:
            mod3n = mod[l + 1].reshape(MOD_ROWS, 1, N_MOD * d)
            x, h = _norm_mod(x, g_ffn1[l + 1], mod3n, lay, 0, 1, g_post=g_post[l])

    outs = [None, None]
    for slot, (xg, off) in enumerate(zip(xs, lay.offsets)):
        n_tok = xg.shape[0] * xg.shape[1]
        outs[order[slot]] = _post_norm(x, g_post[depth - 1], off, n_tok).reshape(xg.shape)
    return tuple(outs)
```

```python
import functools

import jax
import jax.numpy as jnp
from jax import lax
from jax.experimental import pallas as pl
from jax.experimental.pallas import tpu as pltpu

HEAD_DIM = 128
GQA_RATIO = 4
GRID_W = 64
RET_CHUNK = 128
ROPE_THETA = 10000.0
NORM_EPS = 1e-6
GN_EPS = 1e-5
N_MOD = 9
MOD_ROWS = 8
LOG2_E = 1.4426950408889634
BF16_SUBLANES = 16

ATTN_Q_TILE = 512
ATTN_KV_CHUNK = 512
ATTN_UNROLL = 2
ATTN_FAST_BOUND = 40.0
RET_BLOCK = 16

V7X_VMEM_REQUEST_CAP = 60 * 1024 * 1024
VMEM_REQUEST_FLOOR = 16 * 1024 * 1024
CAST_BLOCK_BYTES = 12 * 1024 * 1024

F32 = jnp.float32
BF16 = jnp.bfloat16


def _pick(n, prefs):
    for p in prefs:
        if n % p == 0:
            return p
    return n


def _nbytes(shape, dtype):
    size = 1
    for s in shape:
        size *= s
    return size * jnp.dtype(dtype).itemsize


def _params(semantics, blocks, extra=0):
    need = 2 * sum(_nbytes(s, d) for s, d in blocks) + extra + (4 << 20)
    need = max(VMEM_REQUEST_FLOOR, min(V7X_VMEM_REQUEST_CAP, need))
    return pltpu.CompilerParams(dimension_semantics=semantics, vmem_limit_bytes=int(need))


def _silu(x):
    return x * jax.nn.sigmoid(x)


def _rms(x, eps):
    return x * lax.rsqrt(jnp.mean(x * x, axis=-1, keepdims=True) + eps)


class _Layout:
    def __init__(self, groups):
        self.groups = groups
        self.offsets = []
        self.row0 = []
        off = 0
        row = 0
        for n_seq, seq_len in groups:
            assert off % seq_len == 0, "group offset must be a multiple of its sequence length"
            self.offsets.append(off)
            self.row0.append(row)
            off += n_seq * seq_len
            row += n_seq
        self.total = off
        self.n_rows = row
        self.min_seq = min(s for _, s in groups)
        self.max_seq = max(s for _, s in groups)

    def seq_row(self, t0):
        row = None
        for idx in reversed(range(len(self.groups))):
            here = self.row0[idx] + (t0 - self.offsets[idx]) // self.groups[idx][1]
            row = here if row is None else jnp.where(t0 < self.offsets[idx + 1], here, row)
        return row

    def pos(self, t0):
        p = None
        for idx in reversed(range(len(self.groups))):
            here = (t0 - self.offsets[idx]) % self.groups[idx][1]
            p = here if p is None else jnp.where(t0 < self.offsets[idx + 1], here, p)
        return p


def _mod_kernel(c_ref, w_ref, b_ref, o_ref):
    s = _silu(c_ref[...]).astype(BF16)
    o_ref[...] = jnp.dot(s, w_ref[...].astype(BF16), preferred_element_type=F32) + b_ref[...]


def _cast_kernel(w_ref, o_ref):
    o_ref[...] = w_ref[...].astype(o_ref.dtype)


def _to_bf16(w):
    depth, rows, cols = w.shape
    tr = next((p for p in (1024, 512, 256, 128) if rows % p == 0 and p * cols * 4 <= CAST_BLOCK_BYTES), rows)
    spec = pl.BlockSpec((None, tr, cols), lambda l, i: (l, i, 0))
    return pl.pallas_call(
        _cast_kernel,
        out_shape=jax.ShapeDtypeStruct(w.shape, BF16),
        grid=(depth, rows // tr),
        in_specs=[spec],
        out_specs=spec,
        compiler_params=_params(("parallel", "parallel"), [((tr, cols), F32), ((tr, cols), BF16)]),
        name="to_bf16",
    )(w)


def _modulation(c_rows, w_mod, b_mod):
    depth, d, n = w_mod.shape
    tn = _pick(n, (512, 256, 128))
    return pl.pallas_call(
        _mod_kernel,
        out_shape=jax.ShapeDtypeStruct((depth, MOD_ROWS, n), F32),
        grid=(depth, n // tn),
        in_specs=[
            pl.BlockSpec((MOD_ROWS, d), lambda l, j: (0, 0)),
            pl.BlockSpec((None, d, tn), lambda l, j: (l, 0, j)),
            pl.BlockSpec((None, 1, tn), lambda l, j: (l, 0, j)),
        ],
        out_specs=pl.BlockSpec((None, MOD_ROWS, tn), lambda l, j: (l, 0, j)),
        compiler_params=_params(("parallel", "parallel"), [((d, tn), F32), ((MOD_ROWS, d), F32)],
                                extra=_nbytes((d, tn), BF16)),
        name="adaln_mod",
    )(c_rows, w_mod, b_mod.reshape(depth, 1, n))


def _norm_mod_kernel(x_ref, g_ref, sc_ref, sh_ref, h_ref):
    y = _rms(x_ref[...], NORM_EPS) * g_ref[...]
    h_ref[...] = (y * (1.0 + sc_ref[...]) + sh_ref[...]).astype(h_ref.dtype)


def _post_norm_mod_kernel(x_ref, gp_ref, g_ref, sc_ref, sh_ref, xo_ref, h_ref):
    xo = _rms(x_ref[...], NORM_EPS) * gp_ref[...]
    xo_ref[...] = xo
    y = _rms(xo, NORM_EPS) * g_ref[...]
    h_ref[...] = (y * (1.0 + sc_ref[...]) + sh_ref[...]).astype(h_ref.dtype)


def _post_norm_kernel(x_ref, gp_ref, xo_ref):
    xo_ref[...] = _rms(x_ref[...], NORM_EPS) * gp_ref[...]


def _mod_spec(lay, tm, d, chunk):
    return pl.BlockSpec((None, 1, d), lambda i: (lay.seq_row(i * tm), 0, chunk))


def _norm_mod(x, g, mod3, lay, shift_chunk, scale_chunk, g_post=None):
    t, d = x.shape
    tm = _pick(lay.min_seq, (512, 256, 128))
    row = pl.BlockSpec((tm, d), lambda i: (i, 0))
    vec = pl.BlockSpec((1, d), lambda i: (0, 0))
    blocks = [((tm, d), F32), ((tm, d), BF16)]
    if g_post is None:
        return pl.pallas_call(
            _norm_mod_kernel,
            out_shape=jax.ShapeDtypeStruct((t, d), BF16),
            grid=(t // tm,),
            in_specs=[row, vec, _mod_spec(lay, tm, d, scale_chunk), _mod_spec(lay, tm, d, shift_chunk)],
            out_specs=row,
            compiler_params=_params(("parallel",), blocks, extra=2 * _nbytes((tm, d), F32)),
            name="norm_mod",
        )(x, g.reshape(1, d), mod3, mod3)
    return pl.pallas_call(
        _post_norm_mod_kernel,
        out_shape=(jax.ShapeDtypeStruct((t, d), F32), jax.ShapeDtypeStruct((t, d), BF16)),
        grid=(t // tm,),
        in_specs=[row, vec, vec, _mod_spec(lay, tm, d, scale_chunk), _mod_spec(lay, tm, d, shift_chunk)],
        out_specs=(row, row),
        compiler_params=_params(("parallel",), blocks + [((tm, d), F32)], extra=2 * _nbytes((tm, d), F32)),
        name="post_norm_mod",
    )(x, g_post.reshape(1, d), g.reshape(1, d), mod3, mod3)


def _post_norm(x, g_post, t0, n_tok):
    _, d = x.shape
    tm = _pick(n_tok, (512, 256, 128))
    assert t0 % tm == 0
    return pl.pallas_call(
        _post_norm_kernel,
        out_shape=jax.ShapeDtypeStruct((n_tok, d), F32),
        grid=(n_tok // tm,),
        in_specs=[pl.BlockSpec((tm, d), lambda i: (t0 // tm + i, 0)), pl.BlockSpec((1, d), lambda i: (0, 0))],
        out_specs=pl.BlockSpec((tm, d), lambda i: (i, 0)),
        compiler_params=_params(("parallel",), [((tm, d), F32)] * 2, extra=_nbytes((tm, d), F32)),
        name="post_norm",
    )(x, g_post.reshape(1, d))


def _ffn_up_kernel(h_ref, w1_ref, w3_ref, u_ref):
    h = h_ref[...]
    a = jnp.dot(h, w1_ref[...], preferred_element_type=F32)
    b = jnp.dot(h, w3_ref[...], preferred_element_type=F32)
    u_ref[...] = (_silu(a) * b).astype(u_ref.dtype)


def _ffn_up_first_kernel(h_ref, w1_ref, w3_ref, u_ref, w1b_ref, w3b_ref):
    w1 = w1_ref[...].astype(w1b_ref.dtype)
    w3 = w3_ref[...].astype(w3b_ref.dtype)
    w1b_ref[...] = w1
    w3b_ref[...] = w3
    h = h_ref[...]
    a = jnp.dot(h, w1, preferred_element_type=F32)
    b = jnp.dot(h, w3, preferred_element_type=F32)
    u_ref[...] = (_silu(a) * b).astype(u_ref.dtype)


def _ffn_up_rest_kernel(h_ref, w1_ref, w3_ref, prev_ref, u_ref):
    del prev_ref
    _ffn_up_kernel(h_ref, w1_ref, w3_ref, u_ref)


def _ffn_up(h, w1, w3, layer):
    t, d = h.shape
    f = w1.shape[-1]
    tm = _pick(t, (1024, 512, 256, 128))
    tn = _pick(f, (512, 256, 128))
    ni, nj = t // tm, f // tn
    u_shape = jax.ShapeDtypeStruct((t, f), BF16)
    wb_shape = jax.ShapeDtypeStruct((d, f), BF16)
    w32 = pl.BlockSpec((None, d, tn), lambda i, j: (layer, 0, j))
    wb = pl.BlockSpec((d, tn), lambda i, j: (0, j))
    u, w1b, w3b = pl.pallas_call(
        _ffn_up_first_kernel,
        out_shape=(u_shape, wb_shape, wb_shape),
        grid=(1, nj),
        in_specs=[pl.BlockSpec((tm, d), lambda i, j: (0, 0)), w32, w32],
        out_specs=(pl.BlockSpec((tm, tn), lambda i, j: (0, j)), wb, wb),
        compiler_params=_params(("arbitrary", "arbitrary"),
                                [((tm, d), BF16), ((d, tn), F32), ((d, tn), F32), ((d, tn), BF16),
                                 ((d, tn), BF16), ((tm, tn), BF16)],
                                extra=3 * _nbytes((tm, tn), F32)),
        name="ffn_up_first",
    )(h, w1, w3)
    if ni == 1:
        return u
    return pl.pallas_call(
        _ffn_up_rest_kernel,
        out_shape=u_shape,
        grid=(ni - 1, nj),
        in_specs=[pl.BlockSpec((tm, d), lambda i, j: (i + 1, 0)), wb, wb, pl.BlockSpec(memory_space=pl.ANY)],
        out_specs=pl.BlockSpec((tm, tn), lambda i, j: (i + 1, j)),
        input_output_aliases={3: 0},
        compiler_params=_params(("parallel", "arbitrary"),
                                [((tm, d), BF16), ((d, tn), BF16), ((d, tn), BF16), ((tm, tn), BF16)],
                                extra=3 * _nbytes((tm, tn), F32)),
        name="ffn_up",
    )(h, w1b, w3b, u)


def _ffn_down_kernel(u_ref, w2_ref, x_ref, ga_ref, o_ref):
    acc = jnp.dot(u_ref[...], w2_ref[...], preferred_element_type=F32)
    o_ref[...] = x_ref[...] + (0.5 * ga_ref[...]) * acc


def _ffn_down(u, w2, x, mod3, lay, layer, gate_chunk):
    t, f = u.shape
    d = x.shape[-1]
    tm = _pick(lay.min_seq, (512, 256, 128))
    tn = _pick(d, (512, 256, 128))
    nj = d // tn
    tile = pl.BlockSpec((tm, tn), lambda i, j: (i, j))
    return pl.pallas_call(
        _ffn_down_kernel,
        out_shape=jax.ShapeDtypeStruct((t, d), F32),
        grid=(t // tm, nj),
        in_specs=[
            pl.BlockSpec((tm, f), lambda i, j: (i, 0)),
            pl.BlockSpec((None, f, tn), lambda i, j: (layer, 0, j)),
            tile,
            pl.BlockSpec((None, 1, tn), lambda i, j: (lay.seq_row(i * tm), 0, gate_chunk * nj + j)),
        ],
        out_specs=tile,
        compiler_params=_params(("parallel", "arbitrary"),
                                [((tm, f), BF16), ((f, tn), BF16), ((tm, tn), F32), ((tm, tn), F32)],
                                extra=2 * _nbytes((tm, tn), F32)),
        name="ffn_down",
    )(u, w2, x, mod3)


def _in_proj_kernel(h_ref, w_ref, o_ref):
    acc = jnp.dot(h_ref[...], w_ref[...], preferred_element_type=F32)
    for c in range(o_ref.shape[0]):
        o_ref[c] = acc[:, c * HEAD_DIM:(c + 1) * HEAD_DIM].astype(o_ref.dtype)


def _in_proj_first_kernel(h_ref, w_ref, o_ref, wb_ref):
    w = w_ref[...].astype(wb_ref.dtype)
    wb_ref[...] = w
    acc = jnp.dot(h_ref[...], w, preferred_element_type=F32)
    for c in range(o_ref.shape[0]):
        o_ref[c] = acc[:, c * HEAD_DIM:(c + 1) * HEAD_DIM].astype(o_ref.dtype)


def _in_proj_rest_kernel(h_ref, w_ref, prev_ref, o_ref):
    del prev_ref
    _in_proj_kernel(h_ref, w_ref, o_ref)


def _in_proj(h, w_in, layer):
    t, d = h.shape
    n = w_in.shape[-1]
    tm = _pick(t, (1024, 512, 256, 128))
    tn = _pick(n, (1024, 512, 256, 128))
    tn1 = _pick(n, (512, 256, 128))
    ni = t // tm
    out_shape = jax.ShapeDtypeStruct((n // HEAD_DIM, t, HEAD_DIM), BF16)
    proj, wb = pl.pallas_call(
        _in_proj_first_kernel,
        out_shape=(out_shape, jax.ShapeDtypeStruct((d, n), BF16)),
        grid=(1, n // tn1),
        in_specs=[pl.BlockSpec((tm, d), lambda i, j: (0, 0)),
                  pl.BlockSpec((None, d, tn1), lambda i, j: (layer, 0, j))],
        out_specs=(pl.BlockSpec((tn1 // HEAD_DIM, tm, HEAD_DIM), lambda i, j: (j, 0, 0)),
                   pl.BlockSpec((d, tn1), lambda i, j: (0, j))),
        compiler_params=_params(("arbitrary", "arbitrary"),
                                [((tm, d), BF16), ((d, tn1), F32), ((d, tn1), BF16), ((tm, tn1), BF16)],
                                extra=2 * _nbytes((tm, tn1), F32)),
        name="in_proj_first",
    )(h, w_in)
    if ni == 1:
        return proj
    return pl.pallas_call(
        _in_proj_rest_kernel,
        out_shape=out_shape,
        grid=(ni - 1, n // tn),
        in_specs=[pl.BlockSpec((tm, d), lambda i, j: (i + 1, 0)),
                  pl.BlockSpec((d, tn), lambda i, j: (0, j)),
                  pl.BlockSpec(memory_space=pl.ANY)],
        out_specs=pl.BlockSpec((tn // HEAD_DIM, tm, HEAD_DIM), lambda i, j: (j, i + 1, 0)),
        input_output_aliases={2: 0},
        compiler_params=_params(("parallel", "arbitrary"),
                                [((tm, d), BF16), ((d, tn), BF16), ((tm, tn), BF16)],
                                extra=2 * _nbytes((tm, tn), F32)),
        name="in_proj",
    )(h, wb, proj)


def _rope_tables(n_pos):
    half = HEAD_DIM // 2
    p = jnp.arange(n_pos, dtype=jnp.int32)
    row = (p // GRID_W).astype(F32)
    col = (p % GRID_W).astype(F32)
    inv = ROPE_THETA ** (-jnp.arange(0, half, 2, dtype=F32) / half)
    ang = jnp.concatenate([row[:, None] * inv, col[:, None] * inv], axis=-1)
    cos = jnp.cos(ang)
    sin = jnp.sin(ang)
    cos2 = jnp.repeat(cos, 2, axis=-1)
    sin2 = jnp.stack([-sin, sin], axis=-1).reshape(n_pos, HEAD_DIM)
    return cos2, sin2


def _qk_prep_kernel(p_ref, cos_ref, sin_ref, gain_ref, o_ref, *, n_norm_groups):
    gs, tt, hd = p_ref.shape
    x = p_ref[...].astype(F32).reshape(gs * tt, hd)

    def rotate_and_store(y):
        lane = lax.broadcasted_iota(jnp.int32, y.shape, 1)
        partner = jnp.where(lane % 2 == 0, pltpu.roll(y, hd - 1, axis=1), pltpu.roll(y, 1, axis=1))
        y = y.reshape(gs, tt, hd)
        partner = partner.reshape(gs, tt, hd)
        o_ref[...] = (y * cos_ref[...] + partner * sin_ref[...]).astype(o_ref.dtype)

    @pl.when(pl.program_id(0) < n_norm_groups)
    def _():
        rotate_and_store(_rms(x, NORM_EPS) * gain_ref[...])

    @pl.when(pl.program_id(0) >= n_norm_groups)
    def _():
        rotate_and_store(x * gain_ref[...])


def _qk_prep(proj, cos2, sin2, gain, lay, dims):
    n_attn, n_kv, n_ret, gs = dims
    _, t, hd = proj.shape
    tt = _pick(lay.min_seq, (1024, 512, 256, 128))
    n_groups = (n_attn + n_kv + 2 * n_ret) // gs
    skip_at = (n_attn + n_kv) // gs
    skip = n_kv // gs

    def src(g, i):
        return (jnp.where(g >= skip_at, g + skip, g), i, 0)

    tab = pl.BlockSpec((tt, hd), lambda g, i: (lay.pos(i * tt) // tt, 0))
    par = pl.BlockSpec((None, 1, hd), lambda g, i: (g, 0, 0))
    return pl.pallas_call(
        functools.partial(_qk_prep_kernel, n_norm_groups=(n_attn + n_kv) // gs),
        out_shape=jax.ShapeDtypeStruct((n_groups * gs, t, hd), BF16),
        grid=(n_groups, t // tt),
        in_specs=[pl.BlockSpec((gs, tt, hd), src), tab, tab, par],
        out_specs=pl.BlockSpec((gs, tt, hd), lambda g, i: (g, i, 0)),
        compiler_params=_params(("parallel", "parallel"),
                                [((gs, tt, hd), BF16)] * 2 + [((tt, hd), F32)] * 2,
                                extra=6 * _nbytes((gs, tt, hd), F32)),
        name="qk_prep",
    )(proj, cos2, sin2, gain)


def _v_transpose_kernel(v_ref, o_ref):
    tkv, hd = v_ref.shape
    o_ref[0:hd, :] = v_ref[...].astype(F32).T.astype(o_ref.dtype)
    o_ref[hd:, :] = jnp.ones((o_ref.shape[0] - hd, tkv), o_ref.dtype)


def _v_transpose(proj, slot0, n_slots, tkv):
    _, t, hd = proj.shape
    rows = hd + BF16_SUBLANES
    return pl.pallas_call(
        _v_transpose_kernel,
        out_shape=jax.ShapeDtypeStruct((n_slots, t // tkv, rows, tkv), proj.dtype),
        grid=(n_slots, t // tkv),
        in_specs=[pl.BlockSpec((None, tkv, hd), lambda s, i: (slot0 + s, i, 0))],
        out_specs=pl.BlockSpec((None, None, rows, tkv), lambda s, i: (s, i, 0, 0)),
        compiler_params=_params(("parallel", "parallel"), [((tkv, hd), BF16), ((rows, tkv), BF16)],
                                extra=2 * _nbytes((tkv, hd), F32)),
        name="v_transpose",
    )(proj)


def _attn_kernel(q_ref, k_ref, vt_ref, o_ref, qt_scr, s_scr, mx_scr, p_scr, a_scr, m_scr, acc_scr, kmax_scr):
    g, tq, hd = q_ref.shape
    n_chunks, _, tkv = vt_ref.shape
    nq = g * tq
    qt_scr[...] = q_ref[...].reshape(nq, hd).astype(F32).T.astype(qt_scr.dtype)
    acc_scr[...] = jnp.zeros(acc_scr.shape, F32)
    p_scr[1] = jnp.zeros(p_scr.shape[1:], p_scr.dtype)

    def key_chunk(c):
        return k_ref[pl.ds(pl.multiple_of(c * tkv, tkv), tkv), :]

    @pl.when(pl.program_id(2) == 0)
    def _():
        def chunk_max(c, mx):
            kf = key_chunk(c).astype(F32)
            return jnp.maximum(mx, jnp.max(jnp.sum(kf * kf, axis=1, keepdims=True), axis=0, keepdims=True))

        kmax_scr[...] = lax.fori_loop(0, n_chunks, chunk_max, jnp.zeros((1, 1), F32))

    qf = qt_scr[...].astype(F32)
    bound = jnp.sqrt(jnp.sum(qf * qf, axis=0, keepdims=True) * kmax_scr[...])
    small = jnp.max(bound) < ATTN_FAST_BOUND

    def run_chunks(step):
        unroll = ATTN_UNROLL if n_chunks % ATTN_UNROLL == 0 else 2

        def body(i, carry):
            for u in range(unroll):
                step(unroll * i + u, u % 2)
            return carry

        lax.fori_loop(0, n_chunks // unroll, body, 0)
        if n_chunks % unroll:
            step(jnp.int32(n_chunks - 1), 0)

    @pl.when(small)
    def _():
        def step(c, slot):
            s = jnp.dot(key_chunk(c), qt_scr[...], preferred_element_type=F32)
            p_scr[slot] = jnp.exp2(s - bound).astype(p_scr.dtype)
            acc_scr[...] += jnp.dot(vt_ref[jnp.maximum(c - 1, 0)], p_scr[1 - slot],
                                    preferred_element_type=F32)

        run_chunks(step)
        last = (n_chunks - 1) % 2
        acc_scr[...] += jnp.dot(vt_ref[n_chunks - 1], p_scr[last], preferred_element_type=F32)

    @pl.when(jnp.logical_not(small))
    def _():
        m_scr[...] = jnp.full(m_scr.shape, -jnp.inf, F32)
        a_scr[1] = jnp.ones(a_scr.shape[1:], F32)

        def scores(c, slot):
            s = jnp.dot(key_chunk(c), qt_scr[...], preferred_element_type=F32)
            s_scr[slot] = s
            mx_scr[slot] = jnp.max(s, axis=0, keepdims=True)

        def values(c_prev, slot_prev):
            pv = jnp.dot(vt_ref[c_prev], p_scr[slot_prev], preferred_element_type=F32)
            acc_scr[...] = a_scr[slot_prev] * acc_scr[...] + pv

        def step(c, slot):
            m_prev = m_scr[...]
            m_new = jnp.maximum(m_prev, mx_scr[slot])
            p_scr[slot] = jnp.exp2(s_scr[slot] - m_new).astype(p_scr.dtype)
            a_scr[slot] = jnp.exp2(m_prev - m_new)
            m_scr[...] = m_new
            values(jnp.maximum(c - 1, 0), 1 - slot)
            scores(jnp.minimum(c + 1, n_chunks - 1), 1 - slot)

        scores(0, 0)
        run_chunks(step)
        values(n_chunks - 1, (n_chunks - 1) % 2)

    out_t = acc_scr[0:hd, :] / acc_scr[hd:hd + 1, :]
    for h in range(g):
        o_ref[:, h * hd:(h + 1) * hd] = out_t[:, h * tq:(h + 1) * tq].T.astype(o_ref.dtype)


def _attention(qk, proj, lay, dims):
    n_attn, n_kv, n_ret, gs = dims
    _, t, hd = qk.shape
    g = n_attn // n_kv
    tkv = _pick(lay.min_seq, (ATTN_KV_CHUNK, 256, 128))
    v_t = _v_transpose(proj, n_attn + n_kv, n_kv, tkv)
    out = None
    for (n_seq, seq), off in zip(lay.groups, lay.offsets):
        tq = _pick(seq, (ATTN_Q_TILE, 128))
        nq = seq // tq
        blk0 = off // seq
        n_chunks = seq // tkv

        def q_map(b, kv, i, nq=nq, off=off):
            return (kv, off // tq + b * nq + i, 0)

        def k_map(b, kv, i, blk0=blk0):
            return (n_attn + kv, blk0 + b, 0)

        def v_map(b, kv, i, blk0=blk0):
            return (kv, blk0 + b, 0, 0)

        def o_map(b, kv, i, nq=nq, off=off):
            return (off // tq + b * nq + i, kv)

        kernel = _attn_kernel
        args = [qk, qk, v_t]
        in_specs = [pl.BlockSpec((g, tq, hd), q_map),
                    pl.BlockSpec((None, seq, hd), k_map),
                    pl.BlockSpec((None, n_chunks, hd + BF16_SUBLANES, tkv), v_map)]
        aliases = {}
        if out is not None:
            args.append(out)
            in_specs.append(pl.BlockSpec(memory_space=pl.ANY))
            aliases = {3: 0}
            kernel = _attn_alias_kernel
        out = pl.pallas_call(
            kernel,
            out_shape=jax.ShapeDtypeStruct((t, n_attn * hd), BF16),
            grid=(n_seq, n_kv, nq),
            in_specs=in_specs,
            out_specs=pl.BlockSpec((tq, g * hd), o_map),
            scratch_shapes=[pltpu.VMEM((hd, g * tq), BF16), pltpu.VMEM((2, tkv, g * tq), F32),
                            pltpu.VMEM((2, 1, g * tq), F32),
                            pltpu.VMEM((2, tkv, g * tq), BF16), pltpu.VMEM((2, 1, g * tq), F32),
                            pltpu.VMEM((1, g * tq), F32),
                            pltpu.VMEM((hd + BF16_SUBLANES, g * tq), F32),
                            pltpu.VMEM((1, 1), F32)],
            input_output_aliases=aliases,
            compiler_params=_params(("parallel", "parallel", "arbitrary"),
                                    [((g, tq, hd), BF16), ((seq, hd), BF16), ((seq, hd), BF16),
                                     ((tq, g * hd), BF16)],
                                    extra=2 * _nbytes((hd, g * tq), F32) + 6 * _nbytes((tkv, g * tq), F32)),
            name="attention",
        )(*args)
    return out


def _attn_alias_kernel(q_ref, k_ref, vt_ref, prev_ref, o_ref, *scratch):
    del prev_ref
    _attn_kernel(q_ref, k_ref, vt_ref, o_ref, *scratch)


def _log_sigmoid(x):
    return -(jnp.maximum(-x, 0.0) + jnp.log1p(jnp.exp(-jnp.abs(x))))


def _ret_kernel(q_ref, k_ref, v_ref, g_ref, dec_ref, o_ref, rb_scr, rf_scr):
    seq, hd = q_ref.shape
    c = RET_CHUNK
    nc = seq // c
    lg = _log_sigmoid(dec_ref[...])
    lg_f = lg[0:1, :]
    lg_b = lg[1:2, :]
    ri = lax.broadcasted_iota(jnp.int32, (c, c), 0).astype(F32)
    ci = lax.broadcasted_iota(jnp.int32, (c, c), 1).astype(F32)
    diff = ri - ci
    decay = (jnp.where(diff >= 0, jnp.exp(jnp.where(diff >= 0, diff, 0.0) * lg_f), 0.0)
             + jnp.where(diff <= 0, jnp.exp(jnp.where(diff <= 0, -diff, 0.0) * lg_b), 0.0))
    lane = lax.broadcasted_iota(jnp.int32, (1, c), 1).astype(F32)
    zeta_f = jnp.exp((c - 1.0 - lane) * lg_f)
    zeta_b = jnp.exp(lane * lg_b)
    xi_f = jnp.exp((ri + 1.0) * lg_f)
    xi_b = jnp.exp((c - ri) * lg_b)
    g_f = jnp.exp(c * lg_f)
    g_b = jnp.exp(c * lg_b)

    nb = max(d for d in range(1, RET_BLOCK + 1) if nc % d == 0)
    rows = nb * c

    def chunk_updates(k3, v3, zeta):
        return [jnp.dot((k3[j].astype(F32).T * zeta).astype(v3.dtype), v3[j], preferred_element_type=F32)
                for j in range(nb)]

    def backward(i, r):
        b = nc // nb - 1 - i
        off = pl.multiple_of(b * rows, rows)
        us = chunk_updates(k_ref[pl.ds(off, rows), :].reshape(nb, c, hd),
                           v_ref[pl.ds(off, rows), :].reshape(nb, c, hd), zeta_b)
        for j in reversed(range(nb)):
            rb_scr[b * nb + j] = r.astype(rb_scr.dtype)
            r = g_b * r + us[j]
        return r

    lax.fori_loop(0, nc // nb, backward, jnp.zeros((hd, hd), F32))

    batched_nt = (((2,), (2,)), ((0,), (0,)))
    batched_nn = (((2,), (1,)), ((0,), (0,)))

    def forward(b, r):
        off = pl.multiple_of(b * rows, rows)
        q3 = q_ref[pl.ds(off, rows), :].reshape(nb, c, hd)
        k3 = k_ref[pl.ds(off, rows), :].reshape(nb, c, hd)
        v3 = v_ref[pl.ds(off, rows), :].reshape(nb, c, hd)
        s = lax.dot_general(q3, k3, batched_nt, preferred_element_type=F32)
        o = lax.dot_general((s * decay).astype(v3.dtype), v3, batched_nn, preferred_element_type=F32)
        us = chunk_updates(k3, v3, zeta_f)
        for j in range(nb):
            rf_scr[j] = r.astype(rf_scr.dtype)
            r = g_f * r + us[j]
        o = o + xi_f * lax.dot_general(q3, rf_scr[0:nb], batched_nn, preferred_element_type=F32)
        o = o + xi_b * lax.dot_general(q3, rb_scr[pl.ds(b * nb, nb)], batched_nn, preferred_element_type=F32)
        mu = jnp.mean(o, axis=-1, keepdims=True)
        var = jnp.mean(jnp.square(o - mu), axis=-1, keepdims=True)
        on = (o - mu) * lax.rsqrt(var + GN_EPS)
        gate = g_ref[pl.ds(off, rows), :].astype(F32).reshape(nb, c, hd)
        o_ref[pl.ds(off, rows), :] = (_silu(gate) * on).astype(o_ref.dtype).reshape(rows, hd)
        return r

    lax.fori_loop(0, nc // nb, forward, jnp.zeros((hd, hd), F32))


def _retention(qk, proj, dec, lay, dims):
    n_attn, n_kv, n_ret, gs = dims
    _, t, hd = qk.shape
    q0 = n_attn + n_kv
    k0 = q0 + n_ret
    v0 = n_attn + 2 * n_kv + 2 * n_ret
    g0 = v0 + n_ret
    out = None
    for (n_seq, seq), off in zip(lay.groups, lay.offsets):
        blk0 = off // seq

        def slab(base, blk0=blk0):
            return pl.BlockSpec((None, seq, hd), lambda b, h: (base + h, blk0 + b, 0))

        args = [qk, qk, proj, proj, dec]
        in_specs = [slab(q0), slab(k0), slab(v0), slab(g0),
                    pl.BlockSpec((None, 2, hd), lambda b, h: (h, 0, 0))]
        kernel = _ret_kernel
        aliases = {}
        if out is not None:
            args.append(out)
            in_specs.append(pl.BlockSpec(memory_space=pl.ANY))
            aliases = {5: 0}
            kernel = _ret_alias_kernel
        out = pl.pallas_call(
            kernel,
            out_shape=jax.ShapeDtypeStruct((t, n_ret * hd), BF16),
            grid=(n_seq, n_ret),
            in_specs=in_specs,
            out_specs=pl.BlockSpec((seq, hd), lambda b, h, blk0=blk0: (blk0 + b, h)),
            scratch_shapes=[pltpu.VMEM((seq // RET_CHUNK, hd, hd), BF16),
                            pltpu.VMEM((RET_BLOCK, hd, hd), BF16)],
            input_output_aliases=aliases,
            compiler_params=_params(("parallel", "parallel"), [((seq, hd), BF16)] * 5,
                                    extra=_nbytes((seq, hd), BF16)),
            name="retention",
        )(*args)
    return out


def _ret_alias_kernel(q_ref, k_ref, v_ref, g_ref, dec_ref, prev_ref, o_ref, rb_scr, rf_scr):
    del prev_ref
    _ret_kernel(q_ref, k_ref, v_ref, g_ref, dec_ref, o_ref, rb_scr, rf_scr)


def _out_proj_kernel(oa_ref, or_ref, wa_ref, wr_ref, x_ref, ga_ref, o_ref):
    acc = jnp.dot(oa_ref[...], wa_ref[...], preferred_element_type=F32)
    acc = acc + jnp.dot(or_ref[...], wr_ref[...], preferred_element_type=F32)
    o_ref[...] = x_ref[...] + ga_ref[...] * acc


def _out_proj(o_attn, o_ret, w_out, x, mod3, lay, layer, gate_chunk):
    t, ka = o_attn.shape
    kr = o_ret.shape[-1]
    assert ka == kr
    d = x.shape[-1]
    tm = _pick(lay.min_seq, (1024, 512, 256, 128))
    tn = _pick(d, (512, 256, 128))
    nj = d // tn
    tile = pl.BlockSpec((tm, tn), lambda i, j: (i, j))
    return pl.pallas_call(
        _out_proj_kernel,
        out_shape=jax.ShapeDtypeStruct((t, d), F32),
        grid=(t // tm, nj),
        in_specs=[
            pl.BlockSpec((tm, ka), lambda i, j: (i, 0)),
            pl.BlockSpec((tm, kr), lambda i, j: (i, 0)),
            pl.BlockSpec((None, ka, tn), lambda i, j: (layer, 0, j)),
            pl.BlockSpec((None, kr, tn), lambda i, j: (layer, 1, j)),
            tile,
            pl.BlockSpec((None, 1, tn), lambda i, j: (lay.seq_row(i * tm), 0, gate_chunk * nj + j)),
        ],
        out_specs=tile,
        compiler_params=_params(("parallel", "arbitrary"),
                                [((tm, ka), BF16), ((tm, kr), BF16), ((ka, tn), BF16), ((kr, tn), BF16),
                                 ((tm, tn), F32), ((tm, tn), F32)],
                                extra=2 * _nbytes((tm, tn), F32)),
        name="out_proj",
    )(o_attn, o_ret, w_out, w_out, x, mod3)


def kernel(x_prompt, x_sample, c_prompt, c_sample, w_mod, b_mod, g_ffn1, ffn1_w1, ffn1_w3, ffn1_w2, g_mix, w_in, q_norm_g, k_norm_g, ret_decay_f, ret_decay_b, w_out, g_ffn2, ffn2_w1, ffn2_w3, ffn2_w2, g_post):
    depth, d, _ = w_mod.shape
    hd = HEAD_DIM
    n_attn = d // (2 * hd)
    n_kv = n_attn // GQA_RATIO
    n_ret = d // (2 * hd)
    gs = 4 if n_kv % 4 == 0 else 1
    dims = (n_attn, n_kv, n_ret, gs)

    named = [(x_prompt, c_prompt), (x_sample, c_sample)]
    order = sorted(range(2), key=lambda i: -named[i][0].shape[1])
    xs = [named[i][0] for i in order]
    cs = [named[i][1] for i in order]
    lay = _Layout([(xg.shape[0], xg.shape[1]) for xg in xs])
    x = jnp.concatenate([xg.reshape(-1, d) for xg in xs], axis=0)
    c_rows = jnp.concatenate(cs + [jnp.zeros((MOD_ROWS - lay.n_rows, d), F32)], axis=0)

    mod = _modulation(c_rows, w_mod, b_mod)
    cos2, sin2 = _rope_tables(lay.max_seq)

    w2a, w2b = _to_bf16(ffn1_w2), _to_bf16(ffn2_w2)
    w_out_b = _to_bf16(w_out)

    scale = hd ** -0.5
    ones = jnp.ones((hd,), F32)
    h = None
    for l in range(depth):
        mod3 = mod[l].reshape(MOD_ROWS, 1, N_MOD * d)
        gains = ([q_norm_g[l] * (scale * LOG2_E)] * (n_attn // gs) + [k_norm_g[l]] * (n_kv // gs)
                 + [ones] * (n_ret // gs) + [ones * scale] * (n_ret // gs))
        gain = jnp.stack(gains).reshape(-1, 1, hd)
        dec = jnp.broadcast_to(jnp.stack([ret_decay_f[l], ret_decay_b[l]], axis=1)[:, :, None],
                               (n_ret, 2, hd)).astype(F32)

        if l == 0:
            h = _norm_mod(x, g_ffn1[l], mod3, lay, 0, 1)
        u = _ffn_up(h, ffn1_w1, ffn1_w3, l)
        x = _ffn_down(u, w2a, x, mod3, lay, l, 2)

        h = _norm_mod(x, g_mix[l], mod3, lay, 3, 4)
        proj = _in_proj(h, w_in, l)
        qk = _qk_prep(proj, cos2, sin2, gain, lay, dims)
        o_attn = _attention(qk, proj, lay, dims)
        o_ret = _retention(qk, proj, dec, lay, dims)
        x = _out_proj(o_attn, o_ret, w_out_b, x, mod3, lay, l, 5)

        h = _norm_mod(x, g_ffn2[l], mod3, lay, 6, 7)
        u = _ffn_up(h, ffn2_w1, ffn2_w3, l)
        x = _ffn_down(u, w2b, x, mod3, lay, l, 8)

        if l + 1 < depth:
            mod3n = mod[l + 1].reshape(MOD_ROWS, 1, N_MOD * d)
            x, h = _norm_mod(x, g_ffn1[l + 1], mod3n, lay, 0, 1, g_post=g_post[l])

    outs = [None, None]
    for slot, (xg, off) in enumerate(zip(xs, lay.offsets)):
        n_tok = xg.shape[0] * xg.shape[1]
        outs[order[slot]] = _post_norm(x, g_post[depth - 1], off, n_tok).reshape(xg.shape)
    return tuple(outs)
```

```python
import functools

import jax
import jax.numpy as jnp
from jax import lax
from jax.experimental import pallas as pl
from jax.experimental.pallas import tpu as pltpu

HEAD_DIM = 128
GQA_RATIO = 4
GRID_W = 64
RET_CHUNK = 128
ROPE_THETA = 10000.0
NORM_EPS = 1e-6
GN_EPS = 1e-5
N_MOD = 9
MOD_ROWS = 8
LOG2_E = 1.4426950408889634
BF16_SUBLANES = 16

ATTN_Q_TILE = 512
ATTN_KV_CHUNK = 512
ATTN_UNROLL = 4
ATTN_FAST_BOUND = 40.0
RET_BLOCK = 16

V7X_VMEM_REQUEST_CAP = 60 * 1024 * 1024
VMEM_REQUEST_FLOOR = 16 * 1024 * 1024
CAST_BLOCK_BYTES = 12 * 1024 * 1024

F32 = jnp.float32
BF16 = jnp.bfloat16


def _pick(n, prefs):
    for p in prefs:
        if n % p == 0:
            return p
    return n


def _nbytes(shape, dtype):
    size = 1
    for s in shape:
        size *= s
    return size * jnp.dtype(dtype).itemsize


def _params(semantics, blocks, extra=0):
    need = 2 * sum(_nbytes(s, d) for s, d in blocks) + extra + (4 << 20)
    need = max(VMEM_REQUEST_FLOOR, min(V7X_VMEM_REQUEST_CAP, need))
    return pltpu.CompilerParams(dimension_semantics=semantics, vmem_limit_bytes=int(need))


def _silu(x):
    return x * jax.nn.sigmoid(x)


def _rms(x, eps):
    return x * lax.rsqrt(jnp.mean(x * x, axis=-1, keepdims=True) + eps)


class _Layout:
    def __init__(self, groups):
        self.groups = groups
        self.offsets = []
        self.row0 = []
        off = 0
        row = 0
        for n_seq, seq_len in groups:
            assert off % seq_len == 0, "group offset must be a multiple of its sequence length"
            self.offsets.append(off)
            self.row0.append(row)
            off += n_seq * seq_len
            row += n_seq
        self.total = off
        self.n_rows = row
        self.min_seq = min(s for _, s in groups)
        self.max_seq = max(s for _, s in groups)

    def seq_row(self, t0):
        row = None
        for idx in reversed(range(len(self.groups))):
            here = self.row0[idx] + (t0 - self.offsets[idx]) // self.groups[idx][1]
            row = here if row is None else jnp.where(t0 < self.offsets[idx + 1], here, row)
        return row

    def pos(self, t0):
        p = None
        for idx in reversed(range(len(self.groups))):
            here = (t0 - self.offsets[idx]) % self.groups[idx][1]
            p = here if p is None else jnp.where(t0 < self.offsets[idx + 1], here, p)
        return p


def _mod_kernel(c_ref, w_ref, b_ref, o_ref):
    s = _silu(c_ref[...]).astype(BF16)
    o_ref[...] = jnp.dot(s, w_ref[...].astype(BF16), preferred_element_type=F32) + b_ref[...]


def _cast_kernel(w_ref, o_ref):
    o_ref[...] = w_ref[...].astype(o_ref.dtype)


def _to_bf16(w):
    depth, rows, cols = w.shape
    tr = next((p for p in (1024, 512, 256, 128) if rows % p == 0 and p * cols * 4 <= CAST_BLOCK_BYTES), rows)
    spec = pl.BlockSpec((None, tr, cols), lambda l, i: (l, i, 0))
    return pl.pallas_call(
        _cast_kernel,
        out_shape=jax.ShapeDtypeStruct(w.shape, BF16),
        grid=(depth, rows // tr),
        in_specs=[spec],
        out_specs=spec,
        compiler_params=_params(("parallel", "parallel"), [((tr, cols), F32), ((tr, cols), BF16)]),
        name="to_bf16",
    )(w)


def _modulation(c_rows, w_mod, b_mod):
    depth, d, n = w_mod.shape
    tn = _pick(n, (512, 256, 128))
    return pl.pallas_call(
        _mod_kernel,
        out_shape=jax.ShapeDtypeStruct((depth, MOD_ROWS, n), F32),
        grid=(depth, n // tn),
        in_specs=[
            pl.BlockSpec((MOD_ROWS, d), lambda l, j: (0, 0)),
            pl.BlockSpec((None, d, tn), lambda l, j: (l, 0, j)),
            pl.BlockSpec((None, 1, tn), lambda l, j: (l, 0, j)),
        ],
        out_specs=pl.BlockSpec((None, MOD_ROWS, tn), lambda l, j: (l, 0, j)),
        compiler_params=_params(("parallel", "parallel"), [((d, tn), F32), ((MOD_ROWS, d), F32)],
                                extra=_nbytes((d, tn), BF16)),
        name="adaln_mod",
    )(c_rows, w_mod, b_mod.reshape(depth, 1, n))


def _norm_mod_kernel(x_ref, g_ref, sc_ref, sh_ref, h_ref):
    y = _rms(x_ref[...], NORM_EPS) * g_ref[...]
    h_ref[...] = (y * (1.0 + sc_ref[...]) + sh_ref[...]).astype(h_ref.dtype)


def _post_norm_mod_kernel(x_ref, gp_ref, g_ref, sc_ref, sh_ref, xo_ref, h_ref):
    xo = _rms(x_ref[...], NORM_EPS) * gp_ref[...]
    xo_ref[...] = xo
    y = _rms(xo, NORM_EPS) * g_ref[...]
    h_ref[...] = (y * (1.0 + sc_ref[...]) + sh_ref[...]).astype(h_ref.dtype)


def _post_norm_kernel(x_ref, gp_ref, xo_ref):
    xo_ref[...] = _rms(x_ref[...], NORM_EPS) * gp_ref[...]


def _mod_spec(lay, tm, d, chunk):
    return pl.BlockSpec((None, 1, d), lambda i: (lay.seq_row(i * tm), 0, chunk))


def _norm_mod(x, g, mod3, lay, shift_chunk, scale_chunk, g_post=None):
    t, d = x.shape
    tm = _pick(lay.min_seq, (512, 256, 128))
    row = pl.BlockSpec((tm, d), lambda i: (i, 0))
    vec = pl.BlockSpec((1, d), lambda i: (0, 0))
    blocks = [((tm, d), F32), ((tm, d), BF16)]
    if g_post is None:
        return pl.pallas_call(
            _norm_mod_kernel,
            out_shape=jax.ShapeDtypeStruct((t, d), BF16),
            grid=(t // tm,),
            in_specs=[row, vec, _mod_spec(lay, tm, d, scale_chunk), _mod_spec(lay, tm, d, shift_chunk)],
            out_specs=row,
            compiler_params=_params(("parallel",), blocks, extra=2 * _nbytes((tm, d), F32)),
            name="norm_mod",
        )(x, g.reshape(1, d), mod3, mod3)
    return pl.pallas_call(
        _post_norm_mod_kernel,
        out_shape=(jax.ShapeDtypeStruct((t, d), F32), jax.ShapeDtypeStruct((t, d), BF16)),
        grid=(t // tm,),
        in_specs=[row, vec, vec, _mod_spec(lay, tm, d, scale_chunk), _mod_spec(lay, tm, d, shift_chunk)],
        out_specs=(row, row),
        compiler_params=_params(("parallel",), blocks + [((tm, d), F32)], extra=2 * _nbytes((tm, d), F32)),
        name="post_norm_mod",
    )(x, g_post.reshape(1, d), g.reshape(1, d), mod3, mod3)


def _norm_mod_alias_kernel(x_ref, g_ref, sc_ref, sh_ref, prev_ref, h_ref):
    del prev_ref
    _norm_mod_kernel(x_ref, g_ref, sc_ref, sh_ref, h_ref)


def _norm_mod_groups(xs, g, mod3, lay, shift_chunk, scale_chunk):
    d = xs[0].shape[-1]
    tm = _pick(lay.min_seq, (512, 256, 128))
    vec = pl.BlockSpec((1, d), lambda i: (0, 0))
    out = None
    for xg, off in zip(xs, lay.offsets):
        blk0 = off // tm

        def mod_spec(chunk, blk0=blk0):
            return pl.BlockSpec((None, 1, d), lambda i: (lay.seq_row((blk0 + i) * tm), 0, chunk))

        args = [xg, g.reshape(1, d), mod3, mod3]
        in_specs = [pl.BlockSpec((tm, d), lambda i: (i, 0)), vec, mod_spec(scale_chunk), mod_spec(shift_chunk)]
        kernel, aliases = _norm_mod_kernel, {}
        if out is not None:
            args.append(out)
            in_specs.append(pl.BlockSpec(memory_space=pl.ANY))
            kernel, aliases = _norm_mod_alias_kernel, {4: 0}
        out = pl.pallas_call(
            kernel,
            out_shape=jax.ShapeDtypeStruct((lay.total, d), BF16),
            grid=(xg.shape[0] // tm,),
            in_specs=in_specs,
            out_specs=pl.BlockSpec((tm, d), lambda i, blk0=blk0: (blk0 + i, 0)),
            input_output_aliases=aliases,
            compiler_params=_params(("parallel",), [((tm, d), F32), ((tm, d), BF16)],
                                    extra=2 * _nbytes((tm, d), F32)),
            name="norm_mod",
        )(*args)
    return out


def _post_norm(x, g_post, t0, n_tok):
    _, d = x.shape
    tm = _pick(n_tok, (512, 256, 128))
    assert t0 % tm == 0
    return pl.pallas_call(
        _post_norm_kernel,
        out_shape=jax.ShapeDtypeStruct((n_tok, d), F32),
        grid=(n_tok // tm,),
        in_specs=[pl.BlockSpec((tm, d), lambda i: (t0 // tm + i, 0)), pl.BlockSpec((1, d), lambda i: (0, 0))],
        out_specs=pl.BlockSpec((tm, d), lambda i: (i, 0)),
        compiler_params=_params(("parallel",), [((tm, d), F32)] * 2, extra=_nbytes((tm, d), F32)),
        name="post_norm",
    )(x, g_post.reshape(1, d))


def _ffn_up_kernel(h_ref, w1_ref, w3_ref, u_ref):
    h = h_ref[...]
    a = jnp.dot(h, w1_ref[...], preferred_element_type=F32)
    b = jnp.dot(h, w3_ref[...], preferred_element_type=F32)
    u_ref[...] = (_silu(a) * b).astype(u_ref.dtype)


def _ffn_up_first_kernel(h_ref, w1_ref, w3_ref, u_ref, w1b_ref, w3b_ref):
    w1 = w1_ref[...].astype(w1b_ref.dtype)
    w3 = w3_ref[...].astype(w3b_ref.dtype)
    w1b_ref[...] = w1
    w3b_ref[...] = w3
    h = h_ref[...]
    a = jnp.dot(h, w1, preferred_element_type=F32)
    b = jnp.dot(h, w3, preferred_element_type=F32)
    u_ref[...] = (_silu(a) * b).astype(u_ref.dtype)


def _ffn_up_rest_kernel(h_ref, w1_ref, w3_ref, prev_ref, u_ref):
    del prev_ref
    _ffn_up_kernel(h_ref, w1_ref, w3_ref, u_ref)


def _ffn_up(h, w1, w3, layer):
    t, d = h.shape
    f = w1.shape[-1]
    tm = _pick(t, (1024, 512, 256, 128))
    tn = _pick(f, (512, 256, 128))
    ni, nj = t // tm, f // tn
    u_shape = jax.ShapeDtypeStruct((t, f), BF16)
    wb_shape = jax.ShapeDtypeStruct((d, f), BF16)
    w32 = pl.BlockSpec((None, d, tn), lambda i, j: (layer, 0, j))
    wb = pl.BlockSpec((d, tn), lambda i, j: (0, j))
    u, w1b, w3b = pl.pallas_call(
        _ffn_up_first_kernel,
        out_shape=(u_shape, wb_shape, wb_shape),
        grid=(1, nj),
        in_specs=[pl.BlockSpec((tm, d), lambda i, j: (0, 0)), w32, w32],
        out_specs=(pl.BlockSpec((tm, tn), lambda i, j: (0, j)), wb, wb),
        compiler_params=_params(("arbitrary", "arbitrary"),
                                [((tm, d), BF16), ((d, tn), F32), ((d, tn), F32), ((d, tn), BF16),
                                 ((d, tn), BF16), ((tm, tn), BF16)],
                                extra=3 * _nbytes((tm, tn), F32)),
        name="ffn_up_first",
    )(h, w1, w3)
    if ni == 1:
        return u
    return pl.pallas_call(
        _ffn_up_rest_kernel,
        out_shape=u_shape,
        grid=(ni - 1, nj),
        in_specs=[pl.BlockSpec((tm, d), lambda i, j: (i + 1, 0)), wb, wb, pl.BlockSpec(memory_space=pl.ANY)],
        out_specs=pl.BlockSpec((tm, tn), lambda i, j: (i + 1, j)),
        input_output_aliases={3: 0},
        compiler_params=_params(("parallel", "arbitrary"),
                                [((tm, d), BF16), ((d, tn), BF16), ((d, tn), BF16), ((tm, tn), BF16)],
                                extra=3 * _nbytes((tm, tn), F32)),
        name="ffn_up",
    )(h, w1b, w3b, u)


def _ffn_down_kernel(u_ref, w2_ref, x_ref, ga_ref, o_ref):
    acc = jnp.dot(u_ref[...], w2_ref[...], preferred_element_type=F32)
    o_ref[...] = x_ref[...] + (0.5 * ga_ref[...]) * acc


def _ffn_down_alias_kernel(u_ref, w2_ref, x_ref, ga_ref, prev_ref, o_ref):
    del prev_ref
    _ffn_down_kernel(u_ref, w2_ref, x_ref, ga_ref, o_ref)


def _ffn_down(u, w2, xs, mod3, lay, layer, gate_chunk):
    t, f = u.shape
    d = xs[0][0].shape[-1]
    tm = _pick(lay.min_seq, (512, 256, 128))
    tn = _pick(d, (512, 256, 128))
    nj = d // tn
    out = None
    for xg, off in xs:
        blk0 = off // tm
        args = [u, w2, xg, mod3]
        in_specs = [
            pl.BlockSpec((tm, f), lambda i, j, blk0=blk0: (blk0 + i, 0)),
            pl.BlockSpec((None, f, tn), lambda i, j: (layer, 0, j)),
            pl.BlockSpec((tm, tn), lambda i, j: (i, j)),
            pl.BlockSpec((None, 1, tn),
                         lambda i, j, blk0=blk0: (lay.seq_row((blk0 + i) * tm), 0, gate_chunk * nj + j)),
        ]
        kernel, aliases = _ffn_down_kernel, {}
        if out is not None:
            args.append(out)
            in_specs.append(pl.BlockSpec(memory_space=pl.ANY))
            kernel, aliases = _ffn_down_alias_kernel, {4: 0}
        out = pl.pallas_call(
            kernel,
            out_shape=jax.ShapeDtypeStruct((t, d), F32),
            grid=(xg.shape[0] // tm, nj),
            in_specs=in_specs,
            out_specs=pl.BlockSpec((tm, tn), lambda i, j, blk0=blk0: (blk0 + i, j)),
            input_output_aliases=aliases,
            compiler_params=_params(("parallel", "arbitrary"),
                                    [((tm, f), BF16), ((f, tn), BF16), ((tm, tn), F32), ((tm, tn), F32)],
                                    extra=2 * _nbytes((tm, tn), F32)),
            name="ffn_down",
        )(*args)
    return out


def _in_proj_kernel(h_ref, w_ref, o_ref):
    acc = jnp.dot(h_ref[...], w_ref[...], preferred_element_type=F32)
    for c in range(o_ref.shape[0]):
        o_ref[c] = acc[:, c * HEAD_DIM:(c + 1) * HEAD_DIM].astype(o_ref.dtype)


def _in_proj_first_kernel(h_ref, w_ref, o_ref, wb_ref):
    w = w_ref[...].astype(wb_ref.dtype)
    wb_ref[...] = w
    acc = jnp.dot(h_ref[...], w, preferred_element_type=F32)
    for c in range(o_ref.shape[0]):
        o_ref[c] = acc[:, c * HEAD_DIM:(c + 1) * HEAD_DIM].astype(o_ref.dtype)


def _in_proj_rest_kernel(h_ref, w_ref, prev_ref, o_ref):
    del prev_ref
    _in_proj_kernel(h_ref, w_ref, o_ref)


def _in_proj(h, w_in, layer):
    t, d = h.shape
    n = w_in.shape[-1]
    tm = _pick(t, (1024, 512, 256, 128))
    tn = _pick(n, (1024, 512, 256, 128))
    tn1 = _pick(n, (512, 256, 128))
    ni = t // tm
    out_shape = jax.ShapeDtypeStruct((n // HEAD_DIM, t, HEAD_DIM), BF16)
    proj, wb = pl.pallas_call(
        _in_proj_first_kernel,
        out_shape=(out_shape, jax.ShapeDtypeStruct((d, n), BF16)),
        grid=(1, n // tn1),
        in_specs=[pl.BlockSpec((tm, d), lambda i, j: (0, 0)),
                  pl.BlockSpec((None, d, tn1), lambda i, j: (layer, 0, j))],
        out_specs=(pl.BlockSpec((tn1 // HEAD_DIM, tm, HEAD_DIM), lambda i, j: (j, 0, 0)),
                   pl.BlockSpec((d, tn1), lambda i, j: (0, j))),
        compiler_params=_params(("arbitrary", "arbitrary"),
                                [((tm, d), BF16), ((d, tn1), F32), ((d, tn1), BF16), ((tm, tn1), BF16)],
                                extra=2 * _nbytes((tm, tn1), F32)),
        name="in_proj_first",
    )(h, w_in)
    if ni == 1:
        return proj
    return pl.pallas_call(
        _in_proj_rest_kernel,
        out_shape=out_shape,
        grid=(ni - 1, n // tn),
        in_specs=[pl.BlockSpec((tm, d), lambda i, j: (i + 1, 0)),
                  pl.BlockSpec((d, tn), lambda i, j: (0, j)),
                  pl.BlockSpec(memory_space=pl.ANY)],
        out_specs=pl.BlockSpec((tn // HEAD_DIM, tm, HEAD_DIM), lambda i, j: (j, i + 1, 0)),
        input_output_aliases={2: 0},
        compiler_params=_params(("parallel", "arbitrary"),
                                [((tm, d), BF16), ((d, tn), BF16), ((tm, tn), BF16)],
                                extra=2 * _nbytes((tm, tn), F32)),
        name="in_proj",
    )(h, wb, proj)


def _rope_tables(n_pos):
    half = HEAD_DIM // 2
    p = jnp.arange(n_pos, dtype=jnp.int32)
    row = (p // GRID_W).astype(F32)
    col = (p % GRID_W).astype(F32)
    inv = ROPE_THETA ** (-jnp.arange(0, half, 2, dtype=F32) / half)
    ang = jnp.concatenate([row[:, None] * inv, col[:, None] * inv], axis=-1)
    cos = jnp.cos(ang)
    sin = jnp.sin(ang)
    cos2 = jnp.repeat(cos, 2, axis=-1)
    sin2 = jnp.stack([-sin, sin], axis=-1).reshape(n_pos, HEAD_DIM)
    return cos2, sin2


def _qk_prep_kernel(p_ref, cos_ref, sin_ref, gain_ref, o_ref, *, n_norm_groups):
    gs, tt, hd = p_ref.shape
    x = p_ref[...].astype(F32).reshape(gs * tt, hd)

    def rotate_and_store(y):
        lane = lax.broadcasted_iota(jnp.int32, y.shape, 1)
        partner = jnp.where(lane % 2 == 0, pltpu.roll(y, hd - 1, axis=1), pltpu.roll(y, 1, axis=1))
        y = y.reshape(gs, tt, hd)
        partner = partner.reshape(gs, tt, hd)
        o_ref[...] = (y * cos_ref[...] + partner * sin_ref[...]).astype(o_ref.dtype)

    @pl.when(pl.program_id(0) < n_norm_groups)
    def _():
        rotate_and_store(_rms(x, NORM_EPS) * gain_ref[...])

    @pl.when(pl.program_id(0) >= n_norm_groups)
    def _():
        rotate_and_store(x * gain_ref[...])


def _qk_prep(proj, cos2, sin2, gain, lay, dims):
    n_attn, n_kv, n_ret, gs = dims
    _, t, hd = proj.shape
    tt = _pick(lay.min_seq, (1024, 512, 256, 128))
    n_groups = (n_attn + n_kv + 2 * n_ret) // gs
    skip_at = (n_attn + n_kv) // gs
    skip = n_kv // gs

    def src(g, i):
        return (jnp.where(g >= skip_at, g + skip, g), i, 0)

    tab = pl.BlockSpec((tt, hd), lambda g, i: (lay.pos(i * tt) // tt, 0))
    par = pl.BlockSpec((None, 1, hd), lambda g, i: (g, 0, 0))
    return pl.pallas_call(
        functools.partial(_qk_prep_kernel, n_norm_groups=(n_attn + n_kv) // gs),
        out_shape=jax.ShapeDtypeStruct((n_groups * gs, t, hd), BF16),
        grid=(n_groups, t // tt),
        in_specs=[pl.BlockSpec((gs, tt, hd), src), tab, tab, par],
        out_specs=pl.BlockSpec((gs, tt, hd), lambda g, i: (g, i, 0)),
        compiler_params=_params(("parallel", "parallel"),
                                [((gs, tt, hd), BF16)] * 2 + [((tt, hd), F32)] * 2,
                                extra=6 * _nbytes((gs, tt, hd), F32)),
        name="qk_prep",
    )(proj, cos2, sin2, gain)


def _v_transpose_kernel(v_ref, o_ref):
    tkv, hd = v_ref.shape
    o_ref[0:hd, :] = v_ref[...].astype(F32).T.astype(o_ref.dtype)
    o_ref[hd:, :] = jnp.ones((o_ref.shape[0] - hd, tkv), o_ref.dtype)


def _v_transpose(proj, slot0, n_slots, tkv):
    _, t, hd = proj.shape
    rows = hd + BF16_SUBLANES
    return pl.pallas_call(
        _v_transpose_kernel,
        out_shape=jax.ShapeDtypeStruct((n_slots, t // tkv, rows, tkv), proj.dtype),
        grid=(n_slots, t // tkv),
        in_specs=[pl.BlockSpec((None, tkv, hd), lambda s, i: (slot0 + s, i, 0))],
        out_specs=pl.BlockSpec((None, None, rows, tkv), lambda s, i: (s, i, 0, 0)),
        compiler_params=_params(("parallel", "parallel"), [((tkv, hd), BF16), ((rows, tkv), BF16)],
                                extra=2 * _nbytes((tkv, hd), F32)),
        name="v_transpose",
    )(proj)


def _attn_kernel(q_ref, k_ref, vt_ref, o_ref, qt_scr, s_scr, mx_scr, p_scr, a_scr, m_scr, acc_scr, kmax_scr):
    g, tq, hd = q_ref.shape
    n_chunks, _, tkv = vt_ref.shape
    nq = g * tq
    qt_scr[...] = q_ref[...].reshape(nq, hd).astype(F32).T.astype(qt_scr.dtype)
    acc_scr[...] = jnp.zeros(acc_scr.shape, F32)
    p_scr[1] = jnp.zeros(p_scr.shape[1:], p_scr.dtype)

    def key_chunk(c):
        return k_ref[pl.ds(pl.multiple_of(c * tkv, tkv), tkv), :]

    @pl.when(pl.program_id(2) == 0)
    def _():
        def chunk_max(c, mx):
            kf = key_chunk(c).astype(F32)
            return jnp.maximum(mx, jnp.max(jnp.sum(kf * kf, axis=1, keepdims=True), axis=0, keepdims=True))

        kmax_scr[...] = lax.fori_loop(0, n_chunks, chunk_max, jnp.zeros((1, 1), F32))

    qf = qt_scr[...].astype(F32)
    bound = jnp.sqrt(jnp.sum(qf * qf, axis=0, keepdims=True) * kmax_scr[...])
    small = jnp.max(bound) < ATTN_FAST_BOUND

    def run_chunks(step):
        unroll = ATTN_UNROLL if n_chunks % ATTN_UNROLL == 0 else 2

        def body(i, carry):
            for u in range(unroll):
                step(unroll * i + u, u % 2)
            return carry

        lax.fori_loop(0, n_chunks // unroll, body, 0)
        if n_chunks % unroll:
            step(jnp.int32(n_chunks - 1), 0)

    @pl.when(small)
    def _():
        def step(c, slot):
            s = jnp.dot(key_chunk(c), qt_scr[...], preferred_element_type=F32)
            p_scr[slot] = jnp.exp2(s - bound).astype(p_scr.dtype)
            acc_scr[...] += jnp.dot(vt_ref[jnp.maximum(c - 1, 0)], p_scr[1 - slot],
                                    preferred_element_type=F32)

        run_chunks(step)
        last = (n_chunks - 1) % 2
        acc_scr[...] += jnp.dot(vt_ref[n_chunks - 1], p_scr[last], preferred_element_type=F32)

    @pl.when(jnp.logical_not(small))
    def _():
        m_scr[...] = jnp.full(m_scr.shape, -jnp.inf, F32)
        a_scr[1] = jnp.ones(a_scr.shape[1:], F32)

        def scores(c, slot):
            s = jnp.dot(key_chunk(c), qt_scr[...], preferred_element_type=F32)
            s_scr[slot] = s
            mx_scr[slot] = jnp.max(s, axis=0, keepdims=True)

        def values(c_prev, slot_prev):
            pv = jnp.dot(vt_ref[c_prev], p_scr[slot_prev], preferred_element_type=F32)
            acc_scr[...] = a_scr[slot_prev] * acc_scr[...] + pv

        def step(c, slot):
            m_prev = m_scr[...]
            m_new = jnp.maximum(m_prev, mx_scr[slot])
            p_scr[slot] = jnp.exp2(s_scr[slot] - m_new).astype(p_scr.dtype)
            a_scr[slot] = jnp.exp2(m_prev - m_new)
            m_scr[...] = m_new
            values(jnp.maximum(c - 1, 0), 1 - slot)
            scores(jnp.minimum(c + 1, n_chunks - 1), 1 - slot)

        scores(0, 0)
        run_chunks(step)
        values(n_chunks - 1, (n_chunks - 1) % 2)

    out_t = acc_scr[0:hd, :] / acc_scr[hd:hd + 1, :]
    for h in range(g):
        o_ref[:, h * hd:(h + 1) * hd] = out_t[:, h * tq:(h + 1) * tq].T.astype(o_ref.dtype)


def _attention(qk, proj, lay, dims):
    n_attn, n_kv, n_ret, gs = dims
    _, t, hd = qk.shape
    g = n_attn // n_kv
    tkv = _pick(lay.min_seq, (ATTN_KV_CHUNK, 256, 128))
    v_t = _v_transpose(proj, n_attn + n_kv, n_kv, tkv)
    out = None
    for (n_seq, seq), off in zip(lay.groups, lay.offsets):
        tq = _pick(seq, (ATTN_Q_TILE, 128))
        nq = seq // tq
        blk0 = off // seq
        n_chunks = seq // tkv

        def q_map(b, kv, i, nq=nq, off=off):
            return (kv, off // tq + b * nq + i, 0)

        def k_map(b, kv, i, blk0=blk0):
            return (n_attn + kv, blk0 + b, 0)

        def v_map(b, kv, i, blk0=blk0):
            return (kv, blk0 + b, 0, 0)

        def o_map(b, kv, i, nq=nq, off=off):
            return (off // tq + b * nq + i, kv)

        kernel = _attn_kernel
        args = [qk, qk, v_t]
        in_specs = [pl.BlockSpec((g, tq, hd), q_map),
                    pl.BlockSpec((None, seq, hd), k_map),
                    pl.BlockSpec((None, n_chunks, hd + BF16_SUBLANES, tkv), v_map)]
        aliases = {}
        if out is not None:
            args.append(out)
            in_specs.append(pl.BlockSpec(memory_space=pl.ANY))
            aliases = {3: 0}
            kernel = _attn_alias_kernel
        out = pl.pallas_call(
            kernel,
            out_shape=jax.ShapeDtypeStruct((t, n_attn * hd), BF16),
            grid=(n_seq, n_kv, nq),
            in_specs=in_specs,
            out_specs=pl.BlockSpec((tq, g * hd), o_map),
            scratch_shapes=[pltpu.VMEM((hd, g * tq), BF16), pltpu.VMEM((2, tkv, g * tq), F32),
                            pltpu.VMEM((2, 1, g * tq), F32),
                            pltpu.VMEM((2, tkv, g * tq), BF16), pltpu.VMEM((2, 1, g * tq), F32),
                            pltpu.VMEM((1, g * tq), F32),
                            pltpu.VMEM((hd + BF16_SUBLANES, g * tq), F32),
                            pltpu.VMEM((1, 1), F32)],
            input_output_aliases=aliases,
            compiler_params=_params(("parallel", "parallel", "arbitrary"),
                                    [((g, tq, hd), BF16), ((seq, hd), BF16), ((seq, hd), BF16),
                                     ((tq, g * hd), BF16)],
                                    extra=2 * _nbytes((hd, g * tq), F32) + 6 * _nbytes((tkv, g * tq), F32)),
            name="attention",
        )(*args)
    return out


def _attn_alias_kernel(q_ref, k_ref, vt_ref, prev_ref, o_ref, *scratch):
    del prev_ref
    _attn_kernel(q_ref, k_ref, vt_ref, o_ref, *scratch)


def _log_sigmoid(x):
    return -(jnp.maximum(-x, 0.0) + jnp.log1p(jnp.exp(-jnp.abs(x))))


def _ret_kernel(q_ref, k_ref, v_ref, g_ref, dec_ref, o_ref, rb_scr, rf_scr):
    seq, hd = q_ref.shape
    c = RET_CHUNK
    nc = seq // c
    lg = _log_sigmoid(dec_ref[...])
    lg_f = lg[0:1, :]
    lg_b = lg[1:2, :]
    ri = lax.broadcasted_iota(jnp.int32, (c, c), 0).astype(F32)
    ci = lax.broadcasted_iota(jnp.int32, (c, c), 1).astype(F32)
    diff = ri - ci
    decay = (jnp.where(diff >= 0, jnp.exp(jnp.where(diff >= 0, diff, 0.0) * lg_f), 0.0)
             + jnp.where(diff <= 0, jnp.exp(jnp.where(diff <= 0, -diff, 0.0) * lg_b), 0.0))
    lane = lax.broadcasted_iota(jnp.int32, (1, c), 1).astype(F32)
    zeta_f = jnp.exp((c - 1.0 - lane) * lg_f)
    zeta_b = jnp.exp(lane * lg_b)
    xi_f = jnp.exp((ri + 1.0) * lg_f)
    xi_b = jnp.exp((c - ri) * lg_b)
    g_f = jnp.exp(c * lg_f)
    g_b = jnp.exp(c * lg_b)

    nb = max(d for d in range(1, RET_BLOCK + 1) if nc % d == 0)
    rows = nb * c

    def chunk_updates(k3, v3, zeta):
        return [jnp.dot((k3[j].astype(F32).T * zeta).astype(v3.dtype), v3[j], preferred_element_type=F32)
                for j in range(nb)]

    def backward(i, r):
        b = nc // nb - 1 - i
        off = pl.multiple_of(b * rows, rows)
        us = chunk_updates(k_ref[pl.ds(off, rows), :].reshape(nb, c, hd),
                           v_ref[pl.ds(off, rows), :].reshape(nb, c, hd), zeta_b)
        for j in reversed(range(nb)):
            rb_scr[b * nb + j] = r.astype(rb_scr.dtype)
            r = g_b * r + us[j]
        return r

    lax.fori_loop(0, nc // nb, backward, jnp.zeros((hd, hd), F32))

    batched_nt = (((2,), (2,)), ((0,), (0,)))
    batched_nn = (((2,), (1,)), ((0,), (0,)))

    def forward(b, r):
        off = pl.multiple_of(b * rows, rows)
        q3 = q_ref[pl.ds(off, rows), :].reshape(nb, c, hd)
        k3 = k_ref[pl.ds(off, rows), :].reshape(nb, c, hd)
        v3 = v_ref[pl.ds(off, rows), :].reshape(nb, c, hd)
        s = lax.dot_general(q3, k3, batched_nt, preferred_element_type=F32)
        o = lax.dot_general((s * decay).astype(v3.dtype), v3, batched_nn, preferred_element_type=F32)
        us = chunk_updates(k3, v3, zeta_f)
        for j in range(nb):
            rf_scr[j] = r.astype(rf_scr.dtype)
            r = g_f * r + us[j]
        o = o + xi_f * lax.dot_general(q3, rf_scr[0:nb], batched_nn, preferred_element_type=F32)
        o = o + xi_b * lax.dot_general(q3, rb_scr[pl.ds(b * nb, nb)], batched_nn, preferred_element_type=F32)
        mu = jnp.mean(o, axis=-1, keepdims=True)
        var = jnp.mean(jnp.square(o - mu), axis=-1, keepdims=True)
        on = (o - mu) * lax.rsqrt(var + GN_EPS)
        gate = g_ref[pl.ds(off, rows), :].astype(F32).reshape(nb, c, hd)
        o_ref[pl.ds(off, rows), :] = (_silu(gate) * on).astype(o_ref.dtype).reshape(rows, hd)
        return r

    lax.fori_loop(0, nc // nb, forward, jnp.zeros((hd, hd), F32))


def _retention(qk, proj, dec, lay, dims):
    n_attn, n_kv, n_ret, gs = dims
    _, t, hd = qk.shape
    q0 = n_attn + n_kv
    k0 = q0 + n_ret
    v0 = n_attn + 2 * n_kv + 2 * n_ret
    g0 = v0 + n_ret
    out = None
    for (n_seq, seq), off in zip(lay.groups, lay.offsets):
        blk0 = off // seq

        def slab(base, blk0=blk0):
            return pl.BlockSpec((None, seq, hd), lambda b, h: (base + h, blk0 + b, 0))

        args = [qk, qk, proj, proj, dec]
        in_specs = [slab(q0), slab(k0), slab(v0), slab(g0),
                    pl.BlockSpec((None, 2, hd), lambda b, h: (h, 0, 0))]
        kernel = _ret_kernel
        aliases = {}
        if out is not None:
            args.append(out)
            in_specs.append(pl.BlockSpec(memory_space=pl.ANY))
            aliases = {5: 0}
            kernel = _ret_alias_kernel
        out = pl.pallas_call(
            kernel,
            out_shape=jax.ShapeDtypeStruct((t, n_ret * hd), BF16),
            grid=(n_seq, n_ret),
            in_specs=in_specs,
            out_specs=pl.BlockSpec((seq, hd), lambda b, h, blk0=blk0: (blk0 + b, h)),
            scratch_shapes=[pltpu.VMEM((seq // RET_CHUNK, hd, hd), BF16),
                            pltpu.VMEM((RET_BLOCK, hd, hd), BF16)],
            input_output_aliases=aliases,
            compiler_params=_params(("parallel", "parallel"), [((seq, hd), BF16)] * 5,
                                    extra=_nbytes((seq, hd), BF16)),
            name="retention",
        )(*args)
    return out


def _ret_alias_kernel(q_ref, k_ref, v_ref, g_ref, dec_ref, prev_ref, o_ref, rb_scr, rf_scr):
    del prev_ref
    _ret_kernel(q_ref, k_ref, v_ref, g_ref, dec_ref, o_ref, rb_scr, rf_scr)


def _out_proj_kernel(oa_ref, or_ref, wa_ref, wr_ref, x_ref, ga_ref, o_ref):
    acc = jnp.dot(oa_ref[...], wa_ref[...], preferred_element_type=F32)
    acc = acc + jnp.dot(or_ref[...], wr_ref[...], preferred_element_type=F32)
    o_ref[...] = x_ref[...] + ga_ref[...] * acc


def _out_proj(o_attn, o_ret, w_out, x, mod3, lay, layer, gate_chunk):
    t, ka = o_attn.shape
    kr = o_ret.shape[-1]
    assert ka == kr
    d = x.shape[-1]
    tm = _pick(lay.min_seq, (1024, 512, 256, 128))
    tn = _pick(d, (512, 256, 128))
    nj = d // tn
    tile = pl.BlockSpec((tm, tn), lambda i, j: (i, j))
    return pl.pallas_call(
        _out_proj_kernel,
        out_shape=jax.ShapeDtypeStruct((t, d), F32),
        grid=(t // tm, nj),
        in_specs=[
            pl.BlockSpec((tm, ka), lambda i, j: (i, 0)),
            pl.BlockSpec((tm, kr), lambda i, j: (i, 0)),
            pl.BlockSpec((None, ka, tn), lambda i, j: (layer, 0, j)),
            pl.BlockSpec((None, kr, tn), lambda i, j: (layer, 1, j)),
            tile,
            pl.BlockSpec((None, 1, tn), lambda i, j: (lay.seq_row(i * tm), 0, gate_chunk * nj + j)),
        ],
        out_specs=tile,
        compiler_params=_params(("parallel", "arbitrary"),
                                [((tm, ka), BF16), ((tm, kr), BF16), ((ka, tn), BF16), ((kr, tn), BF16),
                                 ((tm, tn), F32), ((tm, tn), F32)],
                                extra=2 * _nbytes((tm, tn), F32)),
        name="out_proj",
    )(o_attn, o_ret, w_out, w_out, x, mod3)


def kernel(x_prompt, x_sample, c_prompt, c_sample, w_mod, b_mod, g_ffn1, ffn1_w1, ffn1_w3, ffn1_w2, g_mix, w_in, q_norm_g, k_norm_g, ret_decay_f, ret_decay_b, w_out, g_ffn2, ffn2_w1, ffn2_w3, ffn2_w2, g_post):
    depth, d, _ = w_mod.shape
    hd = HEAD_DIM
    n_attn = d // (2 * hd)
    n_kv = n_attn // GQA_RATIO
    n_ret = d // (2 * hd)
    gs = 4 if n_kv % 4 == 0 else 1
    dims = (n_attn, n_kv, n_ret, gs)

    named = [(x_prompt, c_prompt), (x_sample, c_sample)]
    order = sorted(range(2), key=lambda i: -named[i][0].shape[1])
    xs = [named[i][0] for i in order]
    cs = [named[i][1] for i in order]
    lay = _Layout([(xg.shape[0], xg.shape[1]) for xg in xs])
    x_groups = [xg.reshape(-1, d) for xg in xs]
    residual = list(zip(x_groups, lay.offsets))
    c_rows = jnp.concatenate(cs + [jnp.zeros((MOD_ROWS - lay.n_rows, d), F32)], axis=0)

    mod = _modulation(c_rows, w_mod, b_mod)
    cos2, sin2 = _rope_tables(lay.max_seq)

    w2a, w2b = _to_bf16(ffn1_w2), _to_bf16(ffn2_w2)
    w_out_b = _to_bf16(w_out)

    scale = hd ** -0.5
    ones = jnp.ones((hd,), F32)
    h = None
    for l in range(depth):
        mod3 = mod[l].reshape(MOD_ROWS, 1, N_MOD * d)
        gains = ([q_norm_g[l] * (scale * LOG2_E)] * (n_attn // gs) + [k_norm_g[l]] * (n_kv // gs)
                 + [ones] * (n_ret // gs) + [ones * scale] * (n_ret // gs))
        gain = jnp.stack(gains).reshape(-1, 1, hd)
        dec = jnp.broadcast_to(jnp.stack([ret_decay_f[l], ret_decay_b[l]], axis=1)[:, :, None],
                               (n_ret, 2, hd)).astype(F32)

        if l == 0:
            h = _norm_mod_groups(x_groups, g_ffn1[l], mod3, lay, 0, 1)
        u = _ffn_up(h, ffn1_w1, ffn1_w3, l)
        x = _ffn_down(u, w2a, residual, mod3, lay, l, 2)

        h = _norm_mod(x, g_mix[l], mod3, lay, 3, 4)
        proj = _in_proj(h, w_in, l)
        qk = _qk_prep(proj, cos2, sin2, gain, lay, dims)
        o_attn = _attention(qk, proj, lay, dims)
        o_ret = _retention(qk, proj, dec, lay, dims)
        x = _out_proj(o_attn, o_ret, w_out_b, x, mod3, lay, l, 5)

        h = _norm_mod(x, g_ffn2[l], mod3, lay, 6, 7)
        u = _ffn_up(h, ffn2_w1, ffn2_w3, l)
        x = _ffn_down(u, w2b, [(x, 0)], mod3, lay, l, 8)

        if l + 1 < depth:
            mod3n = mod[l + 1].reshape(MOD_ROWS, 1, N_MOD * d)
            x, h = _norm_mod(x, g_ffn1[l + 1], mod3n, lay, 0, 1, g_post=g_post[l])
            residual = [(x, 0)]

    outs = [None, None]
    for slot, (xg, off) in enumerate(zip(xs, lay.offsets)):
        n_tok = xg.shape[0] * xg.shape[1]
        outs[order[slot]] = _post_norm(x, g_post[depth - 1], off, n_tok).reshape(xg.shape)
    return tuple(outs)
```

```python
import functools

import jax
import jax.numpy as jnp
from jax import lax
from jax.experimental import pallas as pl
from jax.experimental.pallas import tpu as pltpu

HEAD_DIM = 128
GQA_RATIO = 4
GRID_W = 64
RET_CHUNK = 128
ROPE_THETA = 10000.0
NORM_EPS = 1e-6
GN_EPS = 1e-5
N_MOD = 9
MOD_ROWS = 8
LOG2_E = 1.4426950408889634
BF16_SUBLANES = 16

ATTN_Q_TILE = 512
ATTN_KV_CHUNK = 512
ATTN_UNROLL = 8
ATTN_FAST_BOUND = 40.0
RET_BLOCK = 16

V7X_VMEM_REQUEST_CAP = 60 * 1024 * 1024
VMEM_REQUEST_FLOOR = 16 * 1024 * 1024
CAST_BLOCK_BYTES = 12 * 1024 * 1024

F32 = jnp.float32
BF16 = jnp.bfloat16


def _pick(n, prefs):
    for p in prefs:
        if n % p == 0:
            return p
    return n


def _nbytes(shape, dtype):
    size = 1
    for s in shape:
        size *= s
    return size * jnp.dtype(dtype).itemsize


def _params(semantics, blocks, extra=0):
    need = 2 * sum(_nbytes(s, d) for s, d in blocks) + extra + (4 << 20)
    need = max(VMEM_REQUEST_FLOOR, min(V7X_VMEM_REQUEST_CAP, need))
    return pltpu.CompilerParams(dimension_semantics=semantics, vmem_limit_bytes=int(need))


def _silu(x):
    return x * jax.nn.sigmoid(x)


def _rms(x, eps):
    return x * lax.rsqrt(jnp.mean(x * x, axis=-1, keepdims=True) + eps)


class _Layout:
    def __init__(self, groups):
        self.groups = groups
        self.offsets = []
        self.row0 = []
        off = 0
        row = 0
        for n_seq, seq_len in groups:
            assert off % seq_len == 0, "group offset must be a multiple of its sequence length"
            self.offsets.append(off)
            self.row0.append(row)
            off += n_seq * seq_len
            row += n_seq
        self.total = off
        self.n_rows = row
        self.min_seq = min(s for _, s in groups)
        self.max_seq = max(s for _, s in groups)

    def seq_row(self, t0):
        row = None
        for idx in reversed(range(len(self.groups))):
            here = self.row0[idx] + (t0 - self.offsets[idx]) // self.groups[idx][1]
            row = here if row is None else jnp.where(t0 < self.offsets[idx + 1], here, row)
        return row

    def pos(self, t0):
        p = None
        for idx in reversed(range(len(self.groups))):
            here = (t0 - self.offsets[idx]) % self.groups[idx][1]
            p = here if p is None else jnp.where(t0 < self.offsets[idx + 1], here, p)
        return p


def _mod_kernel(c_ref, w_ref, b_ref, o_ref):
    s = _silu(c_ref[...]).astype(BF16)
    o_ref[...] = jnp.dot(s, w_ref[...].astype(BF16), preferred_element_type=F32) + b_ref[...]


def _cast_kernel(w_ref, o_ref):
    o_ref[...] = w_ref[...].astype(o_ref.dtype)


def _to_bf16(w):
    depth, rows, cols = w.shape
    tr = next((p for p in (1024, 512, 256, 128) if rows % p == 0 and p * cols * 4 <= CAST_BLOCK_BYTES), rows)
    spec = pl.BlockSpec((None, tr, cols), lambda l, i: (l, i, 0))
    return pl.pallas_call(
        _cast_kernel,
        out_shape=jax.ShapeDtypeStruct(w.shape, BF16),
        grid=(depth, rows // tr),
        in_specs=[spec],
        out_specs=spec,
        compiler_params=_params(("parallel", "parallel"), [((tr, cols), F32), ((tr, cols), BF16)]),
        name="to_bf16",
    )(w)


def _modulation(c_rows, w_mod, b_mod):
    depth, d, n = w_mod.shape
    tn = _pick(n, (512, 256, 128))
    return pl.pallas_call(
        _mod_kernel,
        out_shape=jax.ShapeDtypeStruct((depth, MOD_ROWS, n), F32),
        grid=(depth, n // tn),
        in_specs=[
            pl.BlockSpec((MOD_ROWS, d), lambda l, j: (0, 0)),
            pl.BlockSpec((None, d, tn), lambda l, j: (l, 0, j)),
            pl.BlockSpec((None, 1, tn), lambda l, j: (l, 0, j)),
        ],
        out_specs=pl.BlockSpec((None, MOD_ROWS, tn), lambda l, j: (l, 0, j)),
        compiler_params=_params(("parallel", "parallel"), [((d, tn), F32), ((MOD_ROWS, d), F32)],
                                extra=_nbytes((d, tn), BF16)),
        name="adaln_mod",
    )(c_rows, w_mod, b_mod.reshape(depth, 1, n))


def _norm_mod_kernel(x_ref, g_ref, sc_ref, sh_ref, h_ref):
    y = _rms(x_ref[...], NORM_EPS) * g_ref[...]
    h_ref[...] = (y * (1.0 + sc_ref[...]) + sh_ref[...]).astype(h_ref.dtype)


def _post_norm_mod_kernel(x_ref, gp_ref, g_ref, sc_ref, sh_ref, xo_ref, h_ref):
    xo = _rms(x_ref[...], NORM_EPS) * gp_ref[...]
    xo_ref[...] = xo
    y = _rms(xo, NORM_EPS) * g_ref[...]
    h_ref[...] = (y * (1.0 + sc_ref[...]) + sh_ref[...]).astype(h_ref.dtype)


def _post_norm_kernel(x_ref, gp_ref, xo_ref):
    xo_ref[...] = _rms(x_ref[...], NORM_EPS) * gp_ref[...]


def _mod_spec(lay, tm, d, chunk):
    return pl.BlockSpec((None, 1, d), lambda i: (lay.seq_row(i * tm), 0, chunk))


def _norm_mod(x, g, mod3, lay, shift_chunk, scale_chunk, g_post=None):
    t, d = x.shape
    tm = _pick(lay.min_seq, (512, 256, 128))
    row = pl.BlockSpec((tm, d), lambda i: (i, 0))
    vec = pl.BlockSpec((1, d), lambda i: (0, 0))
    blocks = [((tm, d), F32), ((tm, d), BF16)]
    if g_post is None:
        return pl.pallas_call(
            _norm_mod_kernel,
            out_shape=jax.ShapeDtypeStruct((t, d), BF16),
            grid=(t // tm,),
            in_specs=[row, vec, _mod_spec(lay, tm, d, scale_chunk), _mod_spec(lay, tm, d, shift_chunk)],
            out_specs=row,
            compiler_params=_params(("parallel",), blocks, extra=2 * _nbytes((tm, d), F32)),
            name="norm_mod",
        )(x, g.reshape(1, d), mod3, mod3)
    return pl.pallas_call(
        _post_norm_mod_kernel,
        out_shape=(jax.ShapeDtypeStruct((t, d), F32), jax.ShapeDtypeStruct((t, d), BF16)),
        grid=(t // tm,),
        in_specs=[row, vec, vec, _mod_spec(lay, tm, d, scale_chunk), _mod_spec(lay, tm, d, shift_chunk)],
        out_specs=(row, row),
        compiler_params=_params(("parallel",), blocks + [((tm, d), F32)], extra=2 * _nbytes((tm, d), F32)),
        name="post_norm_mod",
    )(x, g_post.reshape(1, d), g.reshape(1, d), mod3, mod3)


def _norm_mod_alias_kernel(x_ref, g_ref, sc_ref, sh_ref, prev_ref, h_ref):
    del prev_ref
    _norm_mod_kernel(x_ref, g_ref, sc_ref, sh_ref, h_ref)


def _norm_mod_groups(xs, g, mod3, lay, shift_chunk, scale_chunk):
    d = xs[0].shape[-1]
    tm = _pick(lay.min_seq, (512, 256, 128))
    vec = pl.BlockSpec((1, d), lambda i: (0, 0))
    out = None
    for xg, off in zip(xs, lay.offsets):
        blk0 = off // tm

        def mod_spec(chunk, blk0=blk0):
            return pl.BlockSpec((None, 1, d), lambda i: (lay.seq_row((blk0 + i) * tm), 0, chunk))

        args = [xg, g.reshape(1, d), mod3, mod3]
        in_specs = [pl.BlockSpec((tm, d), lambda i: (i, 0)), vec, mod_spec(scale_chunk), mod_spec(shift_chunk)]
        kernel, aliases = _norm_mod_kernel, {}
        if out is not None:
            args.append(out)
            in_specs.append(pl.BlockSpec(memory_space=pl.ANY))
            kernel, aliases = _norm_mod_alias_kernel, {4: 0}
        out = pl.pallas_call(
            kernel,
            out_shape=jax.ShapeDtypeStruct((lay.total, d), BF16),
            grid=(xg.shape[0] // tm,),
            in_specs=in_specs,
            out_specs=pl.BlockSpec((tm, d), lambda i, blk0=blk0: (blk0 + i, 0)),
            input_output_aliases=aliases,
            compiler_params=_params(("parallel",), [((tm, d), F32), ((tm, d), BF16)],
                                    extra=2 * _nbytes((tm, d), F32)),
            name="norm_mod",
        )(*args)
    return out


def _post_norm(x, g_post, t0, n_tok):
    _, d = x.shape
    tm = _pick(n_tok, (512, 256, 128))
    assert t0 % tm == 0
    return pl.pallas_call(
        _post_norm_kernel,
        out_shape=jax.ShapeDtypeStruct((n_tok, d), F32),
        grid=(n_tok // tm,),
        in_specs=[pl.BlockSpec((tm, d), lambda i: (t0 // tm + i, 0)), pl.BlockSpec((1, d), lambda i: (0, 0))],
        out_specs=pl.BlockSpec((tm, d), lambda i: (i, 0)),
        compiler_params=_params(("parallel",), [((tm, d), F32)] * 2, extra=_nbytes((tm, d), F32)),
        name="post_norm",
    )(x, g_post.reshape(1, d))


def _ffn_up_kernel(h_ref, w1_ref, w3_ref, u_ref):
    h = h_ref[...]
    a = jnp.dot(h, w1_ref[...], preferred_element_type=F32)
    b = jnp.dot(h, w3_ref[...], preferred_element_type=F32)
    u_ref[...] = (_silu(a) * b).astype(u_ref.dtype)


def _ffn_up_first_kernel(h_ref, w1_ref, w3_ref, u_ref, w1b_ref, w3b_ref):
    w1 = w1_ref[...].astype(w1b_ref.dtype)
    w3 = w3_ref[...].astype(w3b_ref.dtype)
    w1b_ref[...] = w1
    w3b_ref[...] = w3
    h = h_ref[...]
    a = jnp.dot(h, w1, preferred_element_type=F32)
    b = jnp.dot(h, w3, preferred_element_type=F32)
    u_ref[...] = (_silu(a) * b).astype(u_ref.dtype)


def _ffn_up_rest_kernel(h_ref, w1_ref, w3_ref, prev_ref, u_ref):
    del prev_ref
    _ffn_up_kernel(h_ref, w1_ref, w3_ref, u_ref)


def _ffn_up(h, w1, w3, layer):
    t, d = h.shape
    f = w1.shape[-1]
    tm = _pick(t, (1024, 512, 256, 128))
    tn = _pick(f, (512, 256, 128))
    ni, nj = t // tm, f // tn
    u_shape = jax.ShapeDtypeStruct((t, f), BF16)
    wb_shape = jax.ShapeDtypeStruct((d, f), BF16)
    w32 = pl.BlockSpec((None, d, tn), lambda i, j: (layer, 0, j))
    wb = pl.BlockSpec((d, tn), lambda i, j: (0, j))
    u, w1b, w3b = pl.pallas_call(
        _ffn_up_first_kernel,
        out_shape=(u_shape, wb_shape, wb_shape),
        grid=(1, nj),
        in_specs=[pl.BlockSpec((tm, d), lambda i, j: (0, 0)), w32, w32],
        out_specs=(pl.BlockSpec((tm, tn), lambda i, j: (0, j)), wb, wb),
        compiler_params=_params(("arbitrary", "arbitrary"),
                                [((tm, d), BF16), ((d, tn), F32), ((d, tn), F32), ((d, tn), BF16),
                                 ((d, tn), BF16), ((tm, tn), BF16)],
                                extra=3 * _nbytes((tm, tn), F32)),
        name="ffn_up_first",
    )(h, w1, w3)
    if ni == 1:
        return u
    return pl.pallas_call(
        _ffn_up_rest_kernel,
        out_shape=u_shape,
        grid=(ni - 1, nj),
        in_specs=[pl.BlockSpec((tm, d), lambda i, j: (i + 1, 0)), wb, wb, pl.BlockSpec(memory_space=pl.ANY)],
        out_specs=pl.BlockSpec((tm, tn), lambda i, j: (i + 1, j)),
        input_output_aliases={3: 0},
        compiler_params=_params(("parallel", "arbitrary"),
                                [((tm, d), BF16), ((d, tn), BF16), ((d, tn), BF16), ((tm, tn), BF16)],
                                extra=3 * _nbytes((tm, tn), F32)),
        name="ffn_up",
    )(h, w1b, w3b, u)


def _ffn_down_kernel(u_ref, w2_ref, x_ref, ga_ref, o_ref):
    acc = jnp.dot(u_ref[...], w2_ref[...], preferred_element_type=F32)
    o_ref[...] = x_ref[...] + (0.5 * ga_ref[...]) * acc


def _ffn_down_alias_kernel(u_ref, w2_ref, x_ref, ga_ref, prev_ref, o_ref):
    del prev_ref
    _ffn_down_kernel(u_ref, w2_ref, x_ref, ga_ref, o_ref)


def _ffn_down(u, w2, xs, mod3, lay, layer, gate_chunk):
    t, f = u.shape
    d = xs[0][0].shape[-1]
    tm = _pick(lay.min_seq, (512, 256, 128))
    tn = _pick(d, (512, 256, 128))
    nj = d // tn
    out = None
    for xg, off in xs:
        blk0 = off // tm
        args = [u, w2, xg, mod3]
        in_specs = [
            pl.BlockSpec((tm, f), lambda i, j, blk0=blk0: (blk0 + i, 0)),
            pl.BlockSpec((None, f, tn), lambda i, j: (layer, 0, j)),
            pl.BlockSpec((tm, tn), lambda i, j: (i, j)),
            pl.BlockSpec((None, 1, tn),
                         lambda i, j, blk0=blk0: (lay.seq_row((blk0 + i) * tm), 0, gate_chunk * nj + j)),
        ]
        kernel, aliases = _ffn_down_kernel, {}
        if out is not None:
            args.append(out)
            in_specs.append(pl.BlockSpec(memory_space=pl.ANY))
            kernel, aliases = _ffn_down_alias_kernel, {4: 0}
        out = pl.pallas_call(
            kernel,
            out_shape=jax.ShapeDtypeStruct((t, d), F32),
            grid=(xg.shape[0] // tm, nj),
            in_specs=in_specs,
            out_specs=pl.BlockSpec((tm, tn), lambda i, j, blk0=blk0: (blk0 + i, j)),
            input_output_aliases=aliases,
            compiler_params=_params(("parallel", "arbitrary"),
                                    [((tm, f), BF16), ((f, tn), BF16), ((tm, tn), F32), ((tm, tn), F32)],
                                    extra=2 * _nbytes((tm, tn), F32)),
            name="ffn_down",
        )(*args)
    return out


def _in_proj_kernel(h_ref, w_ref, o_ref):
    acc = jnp.dot(h_ref[...], w_ref[...], preferred_element_type=F32)
    for c in range(o_ref.shape[0]):
        o_ref[c] = acc[:, c * HEAD_DIM:(c + 1) * HEAD_DIM].astype(o_ref.dtype)


def _in_proj_first_kernel(h_ref, w_ref, o_ref, wb_ref):
    w = w_ref[...].astype(wb_ref.dtype)
    wb_ref[...] = w
    acc = jnp.dot(h_ref[...], w, preferred_element_type=F32)
    for c in range(o_ref.shape[0]):
        o_ref[c] = acc[:, c * HEAD_DIM:(c + 1) * HEAD_DIM].astype(o_ref.dtype)


def _in_proj_rest_kernel(h_ref, w_ref, prev_ref, o_ref):
    del prev_ref
    _in_proj_kernel(h_ref, w_ref, o_ref)


def _in_proj(h, w_in, layer):
    t, d = h.shape
    n = w_in.shape[-1]
    tm = _pick(t, (1024, 512, 256, 128))
    tn = _pick(n, (1024, 512, 256, 128))
    tn1 = _pick(n, (512, 256, 128))
    ni = t // tm
    out_shape = jax.ShapeDtypeStruct((n // HEAD_DIM, t, HEAD_DIM), BF16)
    proj, wb = pl.pallas_call(
        _in_proj_first_kernel,
        out_shape=(out_shape, jax.ShapeDtypeStruct((d, n), BF16)),
        grid=(1, n // tn1),
        in_specs=[pl.BlockSpec((tm, d), lambda i, j: (0, 0)),
                  pl.BlockSpec((None, d, tn1), lambda i, j: (layer, 0, j))],
        out_specs=(pl.BlockSpec((tn1 // HEAD_DIM, tm, HEAD_DIM), lambda i, j: (j, 0, 0)),
                   pl.BlockSpec((d, tn1), lambda i, j: (0, j))),
        compiler_params=_params(("arbitrary", "arbitrary"),
                                [((tm, d), BF16), ((d, tn1), F32), ((d, tn1), BF16), ((tm, tn1), BF16)],
                                extra=2 * _nbytes((tm, tn1), F32)),
        name="in_proj_first",
    )(h, w_in)
    if ni == 1:
        return proj
    return pl.pallas_call(
        _in_proj_rest_kernel,
        out_shape=out_shape,
        grid=(ni - 1, n // tn),
        in_specs=[pl.BlockSpec((tm, d), lambda i, j: (i + 1, 0)),
                  pl.BlockSpec((d, tn), lambda i, j: (0, j)),
                  pl.BlockSpec(memory_space=pl.ANY)],
        out_specs=pl.BlockSpec((tn // HEAD_DIM, tm, HEAD_DIM), lambda i, j: (j, i + 1, 0)),
        input_output_aliases={2: 0},
        compiler_params=_params(("parallel", "arbitrary"),
                                [((tm, d), BF16), ((d, tn), BF16), ((tm, tn), BF16)],
                                extra=2 * _nbytes((tm, tn), F32)),
        name="in_proj",
    )(h, wb, proj)


def _rope_tables(n_pos):
    half = HEAD_DIM // 2
    p = jnp.arange(n_pos, dtype=jnp.int32)
    row = (p // GRID_W).astype(F32)
    col = (p % GRID_W).astype(F32)
    inv = ROPE_THETA ** (-jnp.arange(0, half, 2, dtype=F32) / half)
    ang = jnp.concatenate([row[:, None] * inv, col[:, None] * inv], axis=-1)
    cos = jnp.cos(ang)
    sin = jnp.sin(ang)
    cos2 = jnp.repeat(cos, 2, axis=-1)
    sin2 = jnp.stack([-sin, sin], axis=-1).reshape(n_pos, HEAD_DIM)
    return cos2, sin2


def _qk_prep_kernel(p_ref, cos_ref, sin_ref, gain_ref, o_ref, *, n_norm_groups):
    gs, tt, hd = p_ref.shape
    x = p_ref[...].astype(F32).reshape(gs * tt, hd)

    def rotate_and_store(y):
        lane = lax.broadcasted_iota(jnp.int32, y.shape, 1)
        partner = jnp.where(lane % 2 == 0, pltpu.roll(y, hd - 1, axis=1), pltpu.roll(y, 1, axis=1))
        y = y.reshape(gs, tt, hd)
        partner = partner.reshape(gs, tt, hd)
        o_ref[...] = (y * cos_ref[...] + partner * sin_ref[...]).astype(o_ref.dtype)

    @pl.when(pl.program_id(0) < n_norm_groups)
    def _():
        rotate_and_store(_rms(x, NORM_EPS) * gain_ref[...])

    @pl.when(pl.program_id(0) >= n_norm_groups)
    def _():
        rotate_and_store(x * gain_ref[...])


def _qk_prep(proj, cos2, sin2, gain, lay, dims):
    n_attn, n_kv, n_ret, gs = dims
    _, t, hd = proj.shape
    tt = _pick(lay.min_seq, (1024, 512, 256, 128))
    n_groups = (n_attn + n_kv + 2 * n_ret) // gs
    skip_at = (n_attn + n_kv) // gs
    skip = n_kv // gs

    def src(g, i):
        return (jnp.where(g >= skip_at, g + skip, g), i, 0)

    tab = pl.BlockSpec((tt, hd), lambda g, i: (lay.pos(i * tt) // tt, 0))
    par = pl.BlockSpec((None, 1, hd), lambda g, i: (g, 0, 0))
    return pl.pallas_call(
        functools.partial(_qk_prep_kernel, n_norm_groups=(n_attn + n_kv) // gs),
        out_shape=jax.ShapeDtypeStruct((n_groups * gs, t, hd), BF16),
        grid=(n_groups, t // tt),
        in_specs=[pl.BlockSpec((gs, tt, hd), src), tab, tab, par],
        out_specs=pl.BlockSpec((gs, tt, hd), lambda g, i: (g, i, 0)),
        compiler_params=_params(("parallel", "parallel"),
                                [((gs, tt, hd), BF16)] * 2 + [((tt, hd), F32)] * 2,
                                extra=6 * _nbytes((gs, tt, hd), F32)),
        name="qk_prep",
    )(proj, cos2, sin2, gain)


def _v_transpose_kernel(v_ref, o_ref):
    tkv, hd = v_ref.shape
    o_ref[0:hd, :] = v_ref[...].astype(F32).T.astype(o_ref.dtype)
    o_ref[hd:, :] = jnp.ones((o_ref.shape[0] - hd, tkv), o_ref.dtype)


def _v_transpose(proj, slot0, n_slots, tkv):
    _, t, hd = proj.shape
    rows = hd + BF16_SUBLANES
    return pl.pallas_call(
        _v_transpose_kernel,
        out_shape=jax.ShapeDtypeStruct((n_slots, t // tkv, rows, tkv), proj.dtype),
        grid=(n_slots, t // tkv),
        in_specs=[pl.BlockSpec((None, tkv, hd), lambda s, i: (slot0 + s, i, 0))],
        out_specs=pl.BlockSpec((None, None, rows, tkv), lambda s, i: (s, i, 0, 0)),
        compiler_params=_params(("parallel", "parallel"), [((tkv, hd), BF16), ((rows, tkv), BF16)],
                                extra=2 * _nbytes((tkv, hd), F32)),
        name="v_transpose",
    )(proj)


def _attn_kernel(q_ref, k_ref, vt_ref, o_ref, qt_scr, s_scr, mx_scr, p_scr, a_scr, m_scr, acc_scr, kmax_scr):
    g, tq, hd = q_ref.shape
    n_chunks, _, tkv = vt_ref.shape
    nq = g * tq
    qt_scr[...] = q_ref[...].reshape(nq, hd).astype(F32).T.astype(qt_scr.dtype)
    acc_scr[...] = jnp.zeros(acc_scr.shape, F32)
    p_scr[1] = jnp.zeros(p_scr.shape[1:], p_scr.dtype)

    def key_chunk(c):
        return k_ref[pl.ds(pl.multiple_of(c * tkv, tkv), tkv), :]

    @pl.when(pl.program_id(2) == 0)
    def _():
        def chunk_max(c, mx):
            kf = key_chunk(c).astype(F32)
            return jnp.maximum(mx, jnp.max(jnp.sum(kf * kf, axis=1, keepdims=True), axis=0, keepdims=True))

        kmax_scr[...] = lax.fori_loop(0, n_chunks, chunk_max, jnp.zeros((1, 1), F32))

    qf = qt_scr[...].astype(F32)
    bound = jnp.sqrt(jnp.sum(qf * qf, axis=0, keepdims=True) * kmax_scr[...])
    small = jnp.max(bound) < ATTN_FAST_BOUND

    def run_chunks(step):
        unroll = ATTN_UNROLL if n_chunks % ATTN_UNROLL == 0 else 2

        def body(i, carry):
            for u in range(unroll):
                step(unroll * i + u, u % 2)
            return carry

        lax.fori_loop(0, n_chunks // unroll, body, 0)
        if n_chunks % unroll:
            step(jnp.int32(n_chunks - 1), 0)

    @pl.when(small)
    def _():
        def step(c, slot):
            s = jnp.dot(key_chunk(c), qt_scr[...], preferred_element_type=F32)
            p_scr[slot] = jnp.exp2(s - bound).astype(p_scr.dtype)
            acc_scr[...] += jnp.dot(vt_ref[jnp.maximum(c - 1, 0)], p_scr[1 - slot],
                                    preferred_element_type=F32)

        run_chunks(step)
        last = (n_chunks - 1) % 2
        acc_scr[...] += jnp.dot(vt_ref[n_chunks - 1], p_scr[last], preferred_element_type=F32)

    @pl.when(jnp.logical_not(small))
    def _():
        m_scr[...] = jnp.full(m_scr.shape, -jnp.inf, F32)
        a_scr[1] = jnp.ones(a_scr.shape[1:], F32)

        def scores(c, slot):
            s = jnp.dot(key_chunk(c), qt_scr[...], preferred_element_type=F32)
            s_scr[slot] = s
            mx_scr[slot] = jnp.max(s, axis=0, keepdims=True)

        def values(c_prev, slot_prev):
            pv = jnp.dot(vt_ref[c_prev], p_scr[slot_prev], preferred_element_type=F32)
            acc_scr[...] = a_scr[slot_prev] * acc_scr[...] + pv

        def step(c, slot):
            m_prev = m_scr[...]
            m_new = jnp.maximum(m_prev, mx_scr[slot])
            p_scr[slot] = jnp.exp2(s_scr[slot] - m_new).astype(p_scr.dtype)
            a_scr[slot] = jnp.exp2(m_prev - m_new)
            m_scr[...] = m_new
            values(jnp.maximum(c - 1, 0), 1 - slot)
            scores(jnp.minimum(c + 1, n_chunks - 1), 1 - slot)

        scores(0, 0)
        run_chunks(step)
        values(n_chunks - 1, (n_chunks - 1) % 2)

    out_t = acc_scr[0:hd, :] / acc_scr[hd:hd + 1, :]
    for h in range(g):
        o_ref[:, h * hd:(h + 1) * hd] = out_t[:, h * tq:(h + 1) * tq].T.astype(o_ref.dtype)


def _attention(qk, proj, lay, dims):
    n_attn, n_kv, n_ret, gs = dims
    _, t, hd = qk.shape
    g = n_attn // n_kv
    tkv = _pick(lay.min_seq, (ATTN_KV_CHUNK, 256, 128))
    v_t = _v_transpose(proj, n_attn + n_kv, n_kv, tkv)
    out = None
    for (n_seq, seq), off in zip(lay.groups, lay.offsets):
        tq = _pick(seq, (ATTN_Q_TILE, 128))
        nq = seq // tq
        blk0 = off // seq
        n_chunks = seq // tkv

        def q_map(b, kv, i, nq=nq, off=off):
            return (kv, off // tq + b * nq + i, 0)

        def k_map(b, kv, i, blk0=blk0):
            return (n_attn + kv, blk0 + b, 0)

        def v_map(b, kv, i, blk0=blk0):
            return (kv, blk0 + b, 0, 0)

        def o_map(b, kv, i, nq=nq, off=off):
            return (off // tq + b * nq + i, kv)

        kernel = _attn_kernel
        args = [qk, qk, v_t]
        in_specs = [pl.BlockSpec((g, tq, hd), q_map),
                    pl.BlockSpec((None, seq, hd), k_map),
                    pl.BlockSpec((None, n_chunks, hd + BF16_SUBLANES, tkv), v_map)]
        aliases = {}
        if out is not None:
            args.append(out)
            in_specs.append(pl.BlockSpec(memory_space=pl.ANY))
            aliases = {3: 0}
            kernel = _attn_alias_kernel
        out = pl.pallas_call(
            kernel,
            out_shape=jax.ShapeDtypeStruct((t, n_attn * hd), BF16),
            grid=(n_seq, n_kv, nq),
            in_specs=in_specs,
            out_specs=pl.BlockSpec((tq, g * hd), o_map),
            scratch_shapes=[pltpu.VMEM((hd, g * tq), BF16), pltpu.VMEM((2, tkv, g * tq), F32),
                            pltpu.VMEM((2, 1, g * tq), F32),
                            pltpu.VMEM((2, tkv, g * tq), BF16), pltpu.VMEM((2, 1, g * tq), F32),
                            pltpu.VMEM((1, g * tq), F32),
                            pltpu.VMEM((hd + BF16_SUBLANES, g * tq), F32),
                            pltpu.VMEM((1, 1), F32)],
            input_output_aliases=aliases,
            compiler_params=_params(("parallel", "parallel", "arbitrary"),
                                    [((g, tq, hd), BF16), ((seq, hd), BF16), ((seq, hd), BF16),
                                     ((tq, g * hd), BF16)],
                                    extra=2 * _nbytes((hd, g * tq), F32) + 6 * _nbytes((tkv, g * tq), F32)),
            name="attention",
        )(*args)
    return out


def _attn_alias_kernel(q_ref, k_ref, vt_ref, prev_ref, o_ref, *scratch):
    del prev_ref
    _attn_kernel(q_ref, k_ref, vt_ref, o_ref, *scratch)


def _log_sigmoid(x):
    return -(jnp.maximum(-x, 0.0) + jnp.log1p(jnp.exp(-jnp.abs(x))))


def _ret_kernel(q_ref, k_ref, v_ref, g_ref, dec_ref, o_ref, rb_scr, rf_scr):
    seq, hd = q_ref.shape
    c = RET_CHUNK
    nc = seq // c
    lg = _log_sigmoid(dec_ref[...])
    lg_f = lg[0:1, :]
    lg_b = lg[1:2, :]
    ri = lax.broadcasted_iota(jnp.int32, (c, c), 0).astype(F32)
    ci = lax.broadcasted_iota(jnp.int32, (c, c), 1).astype(F32)
    diff = ri - ci
    decay = (jnp.where(diff >= 0, jnp.exp(jnp.where(diff >= 0, diff, 0.0) * lg_f), 0.0)
             + jnp.where(diff <= 0, jnp.exp(jnp.where(diff <= 0, -diff, 0.0) * lg_b), 0.0))
    lane = lax.broadcasted_iota(jnp.int32, (1, c), 1).astype(F32)
    zeta_f = jnp.exp((c - 1.0 - lane) * lg_f)
    zeta_b = jnp.exp(lane * lg_b)
    xi_f = jnp.exp((ri + 1.0) * lg_f)
    xi_b = jnp.exp((c - ri) * lg_b)
    g_f = jnp.exp(c * lg_f)
    g_b = jnp.exp(c * lg_b)

    nb = max(d for d in range(1, RET_BLOCK + 1) if nc % d == 0)
    rows = nb * c

    def chunk_updates(k3, v3, zeta):
        return [jnp.dot((k3[j].astype(F32).T * zeta).astype(v3.dtype), v3[j], preferred_element_type=F32)
                for j in range(nb)]

    def backward(i, r):
        b = nc // nb - 1 - i
        off = pl.multiple_of(b * rows, rows)
        us = chunk_updates(k_ref[pl.ds(off, rows), :].reshape(nb, c, hd),
                           v_ref[pl.ds(off, rows), :].reshape(nb, c, hd), zeta_b)
        for j in reversed(range(nb)):
            rb_scr[b * nb + j] = r.astype(rb_scr.dtype)
            r = g_b * r + us[j]
        return r

    lax.fori_loop(0, nc // nb, backward, jnp.zeros((hd, hd), F32))

    batched_nt = (((2,), (2,)), ((0,), (0,)))
    batched_nn = (((2,), (1,)), ((0,), (0,)))

    def forward(b, r):
        off = pl.multiple_of(b * rows, rows)
        q3 = q_ref[pl.ds(off, rows), :].reshape(nb, c, hd)
        k3 = k_ref[pl.ds(off, rows), :].reshape(nb, c, hd)
        v3 = v_ref[pl.ds(off, rows), :].reshape(nb, c, hd)
        s = lax.dot_general(q3, k3, batched_nt, preferred_element_type=F32)
        o = lax.dot_general((s * decay).astype(v3.dtype), v3, batched_nn, preferred_element_type=F32)
        us = chunk_updates(k3, v3, zeta_f)
        for j in range(nb):
            rf_scr[j] = r.astype(rf_scr.dtype)
            r = g_f * r + us[j]
        o = o + xi_f * lax.dot_general(q3, rf_scr[0:nb], batched_nn, preferred_element_type=F32)
        o = o + xi_b * lax.dot_general(q3, rb_scr[pl.ds(b * nb, nb)], batched_nn, preferred_element_type=F32)
        mu = jnp.mean(o, axis=-1, keepdims=True)
        var = jnp.mean(jnp.square(o - mu), axis=-1, keepdims=True)
        on = (o - mu) * lax.rsqrt(var + GN_EPS)
        gate = g_ref[pl.ds(off, rows), :].astype(F32).reshape(nb, c, hd)
        o_ref[pl.ds(off, rows), :] = (_silu(gate) * on).astype(o_ref.dtype).reshape(rows, hd)
        return r

    lax.fori_loop(0, nc // nb, forward, jnp.zeros((hd, hd), F32))


def _retention(qk, proj, dec, lay, dims):
    n_attn, n_kv, n_ret, gs = dims
    _, t, hd = qk.shape
    q0 = n_attn + n_kv
    k0 = q0 + n_ret
    v0 = n_attn + 2 * n_kv + 2 * n_ret
    g0 = v0 + n_ret
    out = None
    for (n_seq, seq), off in zip(lay.groups, lay.offsets):
        blk0 = off // seq

        def slab(base, blk0=blk0):
            return pl.BlockSpec((None, seq, hd), lambda b, h: (base + h, blk0 + b, 0))

        args = [qk, qk, proj, proj, dec]
        in_specs = [slab(q0), slab(k0), slab(v0), slab(g0),
                    pl.BlockSpec((None, 2, hd), lambda b, h: (h, 0, 0))]
        kernel = _ret_kernel
        aliases = {}
        if out is not None:
            args.append(out)
            in_specs.append(pl.BlockSpec(memory_space=pl.ANY))
            aliases = {5: 0}
            kernel = _ret_alias_kernel
        out = pl.pallas_call(
            kernel,
            out_shape=jax.ShapeDtypeStruct((t, n_ret * hd), BF16),
            grid=(n_seq, n_ret),
            in_specs=in_specs,
            out_specs=pl.BlockSpec((seq, hd), lambda b, h, blk0=blk0: (blk0 + b, h)),
            scratch_shapes=[pltpu.VMEM((seq // RET_CHUNK, hd, hd), BF16),
                            pltpu.VMEM((RET_BLOCK, hd, hd), BF16)],
            input_output_aliases=aliases,
            compiler_params=_params(("parallel", "parallel"), [((seq, hd), BF16)] * 5,
                                    extra=_nbytes((seq, hd), BF16)),
            name="retention",
        )(*args)
    return out


def _ret_alias_kernel(q_ref, k_ref, v_ref, g_ref, dec_ref, prev_ref, o_ref, rb_scr, rf_scr):
    del prev_ref
    _ret_kernel(q_ref, k_ref, v_ref, g_ref, dec_ref, o_ref, rb_scr, rf_scr)


def _out_proj_kernel(oa_ref, or_ref, wa_ref, wr_ref, x_ref, ga_ref, o_ref):
    acc = jnp.dot(oa_ref[...], wa_ref[...], preferred_element_type=F32)
    acc = acc + jnp.dot(or_ref[...], wr_ref[...], preferred_element_type=F32)
    o_ref[...] = x_ref[...] + ga_ref[...] * acc


def _out_proj(o_attn, o_ret, w_out, x, mod3, lay, layer, gate_chunk):
    t, ka = o_attn.shape
    kr = o_ret.shape[-1]
    assert ka == kr
    d = x.shape[-1]
    tm = _pick(lay.min_seq, (1024, 512, 256, 128))
    tn = _pick(d, (512, 256, 128))
    nj = d // tn
    tile = pl.BlockSpec((tm, tn), lambda i, j: (i, j))
    return pl.pallas_call(
        _out_proj_kernel,
        out_shape=jax.ShapeDtypeStruct((t, d), F32),
        grid=(t // tm, nj),
        in_specs=[
            pl.BlockSpec((tm, ka), lambda i, j: (i, 0)),
            pl.BlockSpec((tm, kr), lambda i, j: (i, 0)),
            pl.BlockSpec((None, ka, tn), lambda i, j: (layer, 0, j)),
            pl.BlockSpec((None, kr, tn), lambda i, j: (layer, 1, j)),
            tile,
            pl.BlockSpec((None, 1, tn), lambda i, j: (lay.seq_row(i * tm), 0, gate_chunk * nj + j)),
        ],
        out_specs=tile,
        compiler_params=_params(("parallel", "arbitrary"),
                                [((tm, ka), BF16), ((tm, kr), BF16), ((ka, tn), BF16), ((kr, tn), BF16),
                                 ((tm, tn), F32), ((tm, tn), F32)],
                                extra=2 * _nbytes((tm, tn), F32)),
        name="out_proj",
    )(o_attn, o_ret, w_out, w_out, x, mod3)


def kernel(x_prompt, x_sample, c_prompt, c_sample, w_mod, b_mod, g_ffn1, ffn1_w1, ffn1_w3, ffn1_w2, g_mix, w_in, q_norm_g, k_norm_g, ret_decay_f, ret_decay_b, w_out, g_ffn2, ffn2_w1, ffn2_w3, ffn2_w2, g_post):
    depth, d, _ = w_mod.shape
    hd = HEAD_DIM
    n_attn = d // (2 * hd)
    n_kv = n_attn // GQA_RATIO
    n_ret = d // (2 * hd)
    gs = 4 if n_kv % 4 == 0 else 1
    dims = (n_attn, n_kv, n_ret, gs)

    named = [(x_prompt, c_prompt), (x_sample, c_sample)]
    order = sorted(range(2), key=lambda i: -named[i][0].shape[1])
    xs = [named[i][0] for i in order]
    cs = [named[i][1] for i in order]
    lay = _Layout([(xg.shape[0], xg.shape[1]) for xg in xs])
    x_groups = [xg.reshape(-1, d) for xg in xs]
    residual = list(zip(x_groups, lay.offsets))
    c_rows = jnp.concatenate(cs + [jnp.zeros((MOD_ROWS - lay.n_rows, d), F32)], axis=0)

    mod = _modulation(c_rows, w_mod, b_mod)
    cos2, sin2 = _rope_tables(lay.max_seq)

    w2a, w2b = _to_bf16(ffn1_w2), _to_bf16(ffn2_w2)
    w_out_b = _to_bf16(w_out)

    scale = hd ** -0.5
    ones = jnp.ones((hd,), F32)
    h = None
    for l in range(depth):
        mod3 = mod[l].reshape(MOD_ROWS, 1, N_MOD * d)
        gains = ([q_norm_g[l] * (scale * LOG2_E)] * (n_attn // gs) + [k_norm_g[l]] * (n_kv // gs)
                 + [ones] * (n_ret // gs) + [ones * scale] * (n_ret // gs))
        gain = jnp.stack(gains).reshape(-1, 1, hd)
        dec = jnp.broadcast_to(jnp.stack([ret_decay_f[l], ret_decay_b[l]], axis=1)[:, :, None],
                               (n_ret, 2, hd)).astype(F32)

        if l == 0:
            h = _norm_mod_groups(x_groups, g_ffn1[l], mod3, lay, 0, 1)
        u = _ffn_up(h, ffn1_w1, ffn1_w3, l)
        x = _ffn_down(u, w2a, residual, mod3, lay, l, 2)

        h = _norm_mod(x, g_mix[l], mod3, lay, 3, 4)
        proj = _in_proj(h, w_in, l)
        qk = _qk_prep(proj, cos2, sin2, gain, lay, dims)
        o_attn = _attention(qk, proj, lay, dims)
        o_ret = _retention(qk, proj, dec, lay, dims)
        x = _out_proj(o_attn, o_ret, w_out_b, x, mod3, lay, l, 5)

        h = _norm_mod(x, g_ffn2[l], mod3, lay, 6, 7)
        u = _ffn_up(h, ffn2_w1, ffn2_w3, l)
        x = _ffn_down(u, w2b, [(x, 0)], mod3, lay, l, 8)

        if l + 1 < depth:
            mod3n = mod[l + 1].reshape(MOD_ROWS, 1, N_MOD * d)
            x, h = _norm_mod(x, g_ffn1[l + 1], mod3n, lay, 0, 1, g_post=g_post[l])
            residual = [(x, 0)]

    outs = [None, None]
    for slot, (xg, off) in enumerate(zip(xs, lay.offsets)):
        n_tok = xg.shape[0] * xg.shape[1]
        outs[order[slot]] = _post_norm(x, g_post[depth - 1], off, n_tok).reshape(xg.shape)
    return tuple(outs)
```
